```python
import jax, jax.numpy as jnp
from jax import lax
import numpy as np

D_MODEL = 2048
BATCH = 4
SEQ = 2048
DEPTH = 2
DEC_BATCH = 128
DEC_SEQ = 8
PAST_LEN = 16384
PAGE_SIZE = 128

D_A = D_MODEL // 2
SGU_CHUNK = 128
SGU_GROUPS = 4
SGU_GW = D_A // SGU_GROUPS
D_B = D_MODEL // 2
HG_HEADS = 8
HG_DK = D_B // HG_HEADS
HG_DV = D_B // HG_HEADS
HG_CHUNK = 16
HG_SCALE = HG_DK ** -0.5
PEER_HEADS = 8
PEER_DKEY = 256
PEER_NKEYS = 128
PEER_TOPK = 16
PEER_EXPERTS = PEER_NKEYS * PEER_NKEYS
PEER_BLOCK = 128
ALPHA = (2 * DEPTH) ** 0.25
BETA = (8 * DEPTH) ** -0.25
LN_EPS = 1e-5
RMS_EPS = 1e-6
N_IN = 2 * D_A + 4 * D_B + 2 * D_MODEL
SPLIT_AT = (D_A, 2 * D_A, 2 * D_A + D_B, 2 * D_A + 2 * D_B, 2 * D_A + 3 * D_B,
            2 * D_A + 4 * D_B, 2 * D_A + 4 * D_B + D_MODEL)

kernel_name = 'hawk_gated_sgu_hgrn2_peer_decoder'


def _layer_norm(x, g, b):
    xf = x.astype(jnp.float32)
    mu = jnp.mean(xf, axis=-1, keepdims=True)
    var = jnp.mean(jnp.square(xf - mu), axis=-1, keepdims=True)
    y = (xf - mu) * lax.rsqrt(var + LN_EPS) * g.astype(jnp.float32) + b.astype(jnp.float32)
    return y.astype(x.dtype)


def _sgu_branch(u, v, w_s, b_s, sgu_len):
    B, T, _ = v.shape
    n = T // sgu_len
    mask = jnp.tril(jnp.ones((sgu_len, sgu_len), dtype=bool))
    ws = jnp.where(mask, w_s[:, :sgu_len, :sgu_len], 0).astype(v.dtype)
    vc = v.reshape(B, n, sgu_len, SGU_GROUPS, SGU_GW)
    mixed = jnp.einsum('gts,bnsgc->bntgc', ws, vc) + b_s[:, :sgu_len].T[:, :, None].astype(v.dtype)
    return u * mixed.reshape(B, T, D_A)


def _hgrn2_chunked(q, log_f, k, v, s0, chunk):
    B, T, H, K = q.shape
    V = v.shape[-1]
    n = T // chunk

    def blocks(a):
        return a.astype(jnp.float32).reshape(B, n, chunk, H, a.shape[-1]).transpose(1, 0, 3, 2, 4)

    q, log_f, k, v = blocks(q), blocks(log_f), blocks(k), blocks(v)
    b = jnp.cumsum(log_f, axis=3)
    b_end = b[:, :, :, -1:, :]
    q_dec = q * jnp.exp(b)
    k_dec = k * jnp.exp(-b)
    k_end = k * jnp.exp(b_end - b)
    mask = jnp.tril(jnp.ones((chunk, chunk), dtype=bool))
    attn = jnp.where(mask, jnp.einsum('nbhck,nbhsk->nbhcs', q_dec, k_dec), 0.0)
    o_intra = jnp.einsum('nbhcs,nbhsv->nbhcv', attn, v)

    def step(S, xs):
        q_n, k_n, v_n, g_n = xs
        o_n = jnp.einsum('bhck,bhkv->bhcv', q_n, S)
        S = S * g_n[..., :, None] + jnp.einsum('bhck,bhcv->bhkv', k_n, v_n)
        return S, o_n

    s_fin, o_inter = lax.scan(step, s0.astype(jnp.float32),
                              (q_dec, k_end, v, jnp.exp(b_end[:, :, :, 0, :])))
    o = (o_intra + o_inter).transpose(1, 0, 3, 2, 4).reshape(B, T, H, V)
    return o, s_fin


def _token_mixer(x, s0, sgu_len, hg_chunk, lb, w_in, sgu_ln_g, sgu_ln_b, sgu_w, sgu_b,
                 hg_norm_g, w_pa, w_pb, w_o):
    B, T, _ = x.shape
    z = x @ w_in
    u, v, q, f_pre, i_in, o_gate, gate_a, gate_b = jnp.split(z, SPLIT_AT, axis=-1)
    v_n = _layer_norm(jax.nn.gelu(v), sgu_ln_g, sgu_ln_b)
    y_a = _sgu_branch(jax.nn.gelu(u), v_n, sgu_w, sgu_b, sgu_len)
    f_pre = f_pre.astype(jnp.float32)
    log_f = jnp.logaddexp(jnp.log(lb), jnp.log1p(-lb) + jax.nn.log_sigmoid(f_pre))
    k = (1.0 - lb) * jax.nn.sigmoid(-f_pre)
    heads = lambda a: a.reshape(B, T, HG_HEADS, -1)
    o, s_fin = _hgrn2_chunked(heads(q * HG_SCALE), heads(log_f), heads(k), heads(i_in), s0, hg_chunk)
    o = o * lax.rsqrt(jnp.mean(jnp.square(o), axis=-1, keepdims=True) + RMS_EPS)
    y_b = (o.reshape(B, T, D_B) * hg_norm_g.astype(jnp.float32)
           * jax.nn.silu(o_gate.astype(jnp.float32))).astype(x.dtype)
    merged = jax.nn.sigmoid(gate_a) * (y_a @ w_pa) + jax.nn.sigmoid(gate_b) * (y_b @ w_pb)
    return merged @ w_o, s_fin, v_n


def _peer_ffn(x, w_q, sub_keys, u_tab, v_tab):
    shp = x.shape
    xf = x.reshape(-1, D_MODEL)
    n = xf.shape[0]
    q = (xf @ w_q).astype(jnp.float32).reshape(n, PEER_HEADS, PEER_DKEY)
    half = PEER_DKEY // 2
    sk = sub_keys.astype(jnp.float32)
    s1 = jnp.einsum('nhd,hkd->nhk', q[..., :half], sk[:, 0])
    s2 = jnp.einsum('nhd,hkd->nhk', q[..., half:], sk[:, 1])
    t1, i1 = lax.top_k(s1, PEER_TOPK)
    t2, i2 = lax.top_k(s2, PEER_TOPK)
    cand = (t1[..., :, None] + t2[..., None, :]).reshape(n, PEER_HEADS, PEER_TOPK * PEER_TOPK)
    sc, ci = lax.top_k(cand, PEER_TOPK)
    e1 = jnp.take_along_axis(i1, ci // PEER_TOPK, axis=-1)
    e2 = jnp.take_along_axis(i2, ci % PEER_TOPK, axis=-1)
    experts = (e1 * PEER_NKEYS + e2).reshape(n, PEER_HEADS * PEER_TOPK)
    gate = jax.nn.softmax(sc, axis=-1).reshape(n, PEER_HEADS * PEER_TOPK)
    nb = -(-n // PEER_BLOCK)
    pad = nb * PEER_BLOCK - n
    xp = jnp.pad(xf, ((0, pad), (0, 0))).reshape(nb, PEER_BLOCK, D_MODEL)
    ep = jnp.pad(experts, ((0, pad), (0, 0))).reshape(nb, PEER_BLOCK, -1)
    gp = jnp.pad(gate, ((0, pad), (0, 0))).reshape(nb, PEER_BLOCK, -1)

    def blk(args):
        xb, eb, gb = args
        h = jax.nn.gelu(jnp.einsum('td,tpd->tp', xb, u_tab[eb]))
        return jnp.einsum('tp,tpd->td', (gb * h).astype(v_tab.dtype), v_tab[eb])

    out = lax.map(blk, (xp, ep, gp)).reshape(-1, D_MODEL)[:n]
    return out.reshape(shp).astype(x.dtype)


def _trunk_layer(x, s0, sgu_len, hg_chunk, lb, w_in, sgu_ln_g, sgu_ln_b, sgu_w, sgu_b, hg_norm_g,
                 w_pa, w_pb, w_o, ln1_g, ln1_b, peer_w_q, peer_sub_keys, peer_u, peer_v, ln2_g, ln2_b):
    y, s_fin, v_rows = _token_mixer(x, s0, sgu_len, hg_chunk, lb, w_in, sgu_ln_g, sgu_ln_b, sgu_w,
                                    sgu_b, hg_norm_g, w_pa, w_pb, w_o)
    x = _layer_norm(ALPHA * x + y, ln1_g, ln1_b)
    x = _layer_norm(ALPHA * x + _peer_ffn(x, peer_w_q, peer_sub_keys, peer_u, peer_v), ln2_g, ln2_b)
    return x, s_fin, v_rows


def setup_inputs(seed: int = 0) -> dict:
    key = jax.random.key(seed)
    ks = jax.random.split(key, 21)
    nrm = lambda k, shape, s: jax.random.normal(k, shape, jnp.float32) * s
    gain = lambda k, shape: 1.0 + 0.02 * jax.random.normal(k, shape, jnp.float32)
    return {
        'x_prompt': nrm(ks[0], (BATCH, SEQ, D_MODEL), 1.0),
        'x_sample': nrm(ks[1], (DEC_BATCH, DEC_SEQ, D_MODEL), 1.0),
        'state_hgrn': nrm(ks[2], (DEPTH, DEC_BATCH, HG_HEADS, HG_DK, HG_DV), 0.1),
        'w_in': nrm(ks[3], (DEPTH, D_MODEL, N_IN), D_MODEL ** -0.5),
        'sgu_ln_g': gain(ks[4], (DEPTH, D_A)),
        'sgu_ln_b': nrm(ks[5], (DEPTH, D_A), 0.02),
        'sgu_w': nrm(ks[6], (DEPTH, SGU_GROUPS, SGU_CHUNK, SGU_CHUNK), SGU_CHUNK ** -0.5),
        'sgu_b': gain(ks[7], (DEPTH, SGU_GROUPS, SGU_CHUNK)),
        'hg_lb_logits': nrm(ks[8], (DEPTH, D_B), 1.0),
        'hg_norm_g': gain(ks[9], (DEPTH, D_B)),
        'w_branch_a': nrm(ks[10], (DEPTH, D_A, D_MODEL), BETA * D_A ** -0.5),
        'w_branch_b': nrm(ks[11], (DEPTH, D_B, D_MODEL), BETA * D_B ** -0.5),
        'w_out': nrm(ks[12], (DEPTH, D_MODEL, D_MODEL), BETA * D_MODEL ** -0.5),
        'ln1_g': gain(ks[13], (DEPTH, D_MODEL)),
        'ln1_b': nrm(ks[14], (DEPTH, D_MODEL), 0.02),
        'peer_w_q': nrm(ks[15], (DEPTH, D_MODEL, PEER_HEADS * PEER_DKEY), D_MODEL ** -0.5),
        'peer_sub_keys': nrm(ks[16], (DEPTH, PEER_HEADS, 2, PEER_NKEYS, PEER_DKEY // 2),
                             (PEER_DKEY // 2) ** -0.5),
        'peer_u': nrm(ks[17], (DEPTH, PEER_EXPERTS, D_MODEL), D_MODEL ** -0.5),
        'peer_v': nrm(ks[18], (DEPTH, PEER_EXPERTS, D_MODEL), BETA),
        'ln2_g': gain(ks[19], (DEPTH, D_MODEL)),
        'ln2_b': nrm(ks[20], (DEPTH, D_MODEL), 0.02),
    }


def reference(x_prompt, x_sample, state_hgrn, w_in, sgu_ln_g, sgu_ln_b, sgu_w, sgu_b, hg_lb_logits,
              hg_norm_g, w_branch_a, w_branch_b, w_out, ln1_g, ln1_b, peer_w_q, peer_sub_keys,
              peer_u, peer_v, ln2_g, ln2_b):
    lb_all = jnp.cumsum(jax.nn.softmax(hg_lb_logits.astype(jnp.float32), axis=0), axis=0)
    lb_all = lb_all - lb_all[:1]
    h_p, h_s = x_prompt, x_sample
    s_p0 = jnp.zeros((x_prompt.shape[0], HG_HEADS, HG_DK, HG_DV), jnp.float32)
    dec_seq = x_sample.shape[1]
    new_sp, new_ss, new_v = [], [], []
    for l in range(DEPTH):
        p = (w_in[l], sgu_ln_g[l], sgu_ln_b[l], sgu_w[l], sgu_b[l], hg_norm_g[l], w_branch_a[l],
             w_branch_b[l], w_out[l], ln1_g[l], ln1_b[l], peer_w_q[l], peer_sub_keys[l],
             peer_u[l], peer_v[l], ln2_g[l], ln2_b[l])
        h_p, s_p, _ = _trunk_layer(h_p, s_p0, SGU_CHUNK, HG_CHUNK, lb_all[l], *p)
        h_s, s_s, v_s = _trunk_layer(h_s, state_hgrn[l], dec_seq, dec_seq, lb_all[l], *p)
        new_sp.append(s_p)
        new_ss.append(s_s)
        new_v.append(v_s)
    return (h_p, h_s, jnp.stack(new_sp), jnp.stack(new_ss), jnp.stack(new_v))
```

```python
import functools

import jax
import jax.numpy as jnp
from jax import lax
from jax.experimental import pallas as pl
from jax.experimental.pallas import tpu as pltpu

F32 = jnp.float32
BF16 = jnp.bfloat16

SGU_GROUPS = 4
HG_HEADS = 8
PEER_HEADS = 8
PEER_TOPK = 16
LN_EPS = 1e-5
RMS_EPS = 1e-6
ROWS = 128
HG_SUB = 16
MIB = 1024 * 1024

_NT = (((1,), (1,)), ((), ()))
_TN = (((0,), (0,)), ((), ()))


def _params(sem, vmem_mib):
    return pltpu.CompilerParams(dimension_semantics=sem, vmem_limit_bytes=vmem_mib * MIB)


def _gelu(x):
    return 0.5 * x * (1.0 + jnp.tanh(0.7978845608028654 * (x + 0.044715 * (x * x * x))))


def _sigmoid(x):
    return 1.0 / (1.0 + jnp.exp(-x))


def _layer_norm(x, g, b):
    mu = jnp.mean(x, axis=-1, keepdims=True)
    xc = x - mu
    var = jnp.mean(xc * xc, axis=-1, keepdims=True)
    return xc * lax.rsqrt(var + LN_EPS) * g + b


def _bdot(a, b, dims=None):
    a = a.astype(BF16)
    b = b.astype(BF16)
    if dims is None:
        return jnp.dot(a, b, preferred_element_type=F32)
    return lax.dot_general(a, b, dims, preferred_element_type=F32)


def _matmul_kernel(x_ref, w_ref, o_ref):
    o_ref[...] = jnp.dot(x_ref[...], w_ref[...], preferred_element_type=F32)


def _matmul(x, w, tm, tn, name):
    m, k = x.shape
    n = w.shape[1]
    return pl.pallas_call(
        _matmul_kernel,
        grid=(n // tn, m // tm),
        in_specs=[pl.BlockSpec((tm, k), lambda j, i: (i, 0)),
                  pl.BlockSpec((k, tn), lambda j, i: (0, j))],
        out_specs=pl.BlockSpec((tm, tn), lambda j, i: (i, j)),
        out_shape=jax.ShapeDtypeStruct((m, n), F32),
        compiler_params=_params(("parallel", "parallel"), 40),
        name=name,
    )(x, w)


def _forget_lower_bound(logits_ref, layer):
    lg = logits_ref[...]
    e = jnp.exp(lg - jnp.max(lg, axis=0, keepdims=True))
    p = e / jnp.sum(e, axis=0, keepdims=True)
    return jnp.sum(p[1:layer + 1], axis=0, keepdims=True)


def _hgrn_gates(fp, lb):
    e = jnp.exp(-jnp.abs(fp))
    log_sig = jnp.minimum(fp, 0.0) - jnp.log1p(e)
    sig_neg = jnp.where(fp >= 0.0, e, 1.0) / (1.0 + e)
    if lb is None:
        return log_sig, sig_neg
    k = (1.0 - lb) * sig_neg
    pos = lb > 0.0
    a = jnp.log(jnp.where(pos, lb, 1.0))
    b = jnp.log1p(-lb) + log_sig
    lse = jnp.maximum(a, b) + jnp.log1p(jnp.exp(-jnp.abs(a - b)))
    return jnp.where(pos, lse, b), k


def _segment_cumsum(t_ref, x):
    t = t_ref[...]
    hi = x.astype(BF16)
    r1 = x - hi.astype(F32)
    mid = r1.astype(BF16)
    lo = (r1 - mid.astype(F32)).astype(BF16)
    dot = functools.partial(jnp.dot, preferred_element_type=F32)
    return dot(t, hi) + dot(t, mid) + dot(t, lo)


def _sgu(u, v, lng_ref, lnb_ref, ws_ref, bias_ref, ya_ref):
    vn = _layer_norm(_gelu(v), lng_ref[...], lnb_ref[...])
    vnb = vn.astype(BF16)
    gw = vn.shape[1] // SGU_GROUPS
    for g in range(SGU_GROUPS):
        cs = slice(g * gw, (g + 1) * gw)
        mixed = jnp.dot(ws_ref[g], vnb[:, cs], preferred_element_type=F32) + bias_ref[:, cs]
        ya_ref[:, cs] = (_gelu(u[:, cs]) * mixed).astype(BF16)
    return vn


def _hgrn_out(o, gn, og):
    o = o * lax.rsqrt(jnp.mean(o * o, axis=-1, keepdims=True) + RMS_EPS)
    return (o * gn * (og * _sigmoid(og))).astype(BF16)


def _rows_at(b_ref, size, offset, width):
    parts = []
    for j in range(ROWS // size):
        r = j * size + offset
        if r < 0:
            parts.append(jnp.zeros((size, width), F32))
        else:
            parts.append(jnp.broadcast_to(b_ref[pl.ds(r, 1), :], (size, width)))
    return parts[0] if len(parts) == 1 else jnp.concatenate(parts, axis=0)


def _mixer_prompt_kernel(layer, u_ref, v_ref, q_ref, f_ref, i_ref, og_ref, lng_ref, lnb_ref,
                         ws_ref, bias_ref, tri_ref, lbl_ref, gn_ref,
                         ya_ref, yb_ref, st_ref, stt_ref, b_ref):
    c = pl.program_id(1)
    width = q_ref.shape[1]
    dk = width // HG_HEADS

    @pl.when(c == 0)
    def _():
        stt_ref[...] = jnp.zeros_like(stt_ref)

    _sgu(u_ref[...], v_ref[...], lng_ref, lnb_ref, ws_ref, bias_ref, ya_ref)

    lb = None if layer == 0 else _forget_lower_bound(lbl_ref, layer)
    lf, k = _hgrn_gates(f_ref[...], lb)
    qs = q_ref[...] * (dk ** -0.5)
    b = _segment_cumsum(tri_ref, lf)
    b_ref[...] = b

    e_sub = b - _rows_at(b_ref, HG_SUB, -1, width)
    q_lvl = [qs * jnp.exp(e_sub)]
    k_lvl = [k * jnp.exp(-e_sub)]
    size = HG_SUB
    sizes = []
    while size < ROWS:
        sizes.append(size)
        q_lvl.append(q_lvl[0] if size == HG_SUB else qs * jnp.exp(b - _rows_at(b_ref, size, -1, width)))
        k_lvl.append(k * jnp.exp(_rows_at(b_ref, size, size - 1, width) - b))
        size *= 2
    qd = qs * jnp.exp(b)
    b_end = b_ref[pl.ds(ROWS - 1, 1), :]
    ke = k * jnp.exp(b_end - b)
    g_end = jnp.exp(b_end)

    row = lax.broadcasted_iota(jnp.int32, (ROWS, ROWS), 0)
    col = lax.broadcasted_iota(jnp.int32, (ROWS, ROWS), 1)
    masks = [(row // HG_SUB == col // HG_SUB) & (col <= row)]
    for sz in sizes:
        masks.append(((row // sz) % 2 == 1) & (col // sz == row // sz - 1))

    vv = i_ref[...]
    gn = gn_ref[...]
    og = og_ref[...]
    for h in range(HG_HEADS):
        hs = slice(h * dk, (h + 1) * dk)
        attn = jnp.zeros((ROWS, ROWS), F32)
        for ql, kl, mk in zip(q_lvl, k_lvl, masks):
            attn = jnp.where(mk, _bdot(ql[:, hs], kl[:, hs], _NT), attn)
        vh = vv[:, hs]
        stt = stt_ref[h]
        o = _bdot(attn, vh) + _bdot(qd[:, hs], stt, _NT)
        stt_ref[h] = stt * g_end[:, hs] + _bdot(vh, ke[:, hs], _TN)
        yb_ref[:, hs] = _hgrn_out(o, gn[:, hs], og[:, hs])

    @pl.when(c == pl.num_programs(1) - 1)
    def _():
        for h in range(HG_HEADS):
            st_ref[0, h] = stt_ref[h].T


def _mixer_sample_kernel(layer, seq_len, u_ref, v_ref, q_ref, f_ref, i_ref, og_ref, lng_ref, lnb_ref,
                         ws_ref, bias_ref, tri_ref, lbl_ref, gn_ref, s0_ref,
                         ya_ref, yb_ref, vn_ref, st_ref, b_ref, oi_ref):
    width = q_ref.shape[1]
    dk = width // HG_HEADS
    n_seq = ROWS // seq_len

    vn_ref[...] = _sgu(u_ref[...], v_ref[...], lng_ref, lnb_ref, ws_ref, bias_ref, ya_ref)

    lb = None if layer == 0 else _forget_lower_bound(lbl_ref, layer)
    lf, k = _hgrn_gates(f_ref[...], lb)
    qs = q_ref[...] * (dk ** -0.5)
    b = _segment_cumsum(tri_ref, lf)
    b_ref[...] = b
    qd = qs * jnp.exp(b)
    k0 = k * jnp.exp(-b)
    ke = k * jnp.exp(_rows_at(b_ref, seq_len, seq_len - 1, width) - b)

    row = lax.broadcasted_iota(jnp.int32, (ROWS, ROWS), 0)
    col = lax.broadcasted_iota(jnp.int32, (ROWS, ROWS), 1)
    causal = (row // seq_len == col // seq_len) & (col <= row)
    seq_of_row = lax.broadcasted_iota(jnp.int32, (ROWS, dk), 0) // seq_len

    vv = i_ref[...]
    gn = gn_ref[...]
    og = og_ref[...]
    for h in range(HG_HEADS):
        hs = slice(h * dk, (h + 1) * dk)
        attn = jnp.where(causal, _bdot(qd[:, hs], k0[:, hs], _NT), 0.0)
        vh = vv[:, hs]
        qdh = qd[:, hs].astype(BF16)
        keh = ke[:, hs].astype(BF16)
        oi_ref[...] = _bdot(attn, vh)

        def per_seq(s, carry, h=h, hs=hs, vh=vh, qdh=qdh, keh=keh):
            mine = seq_of_row == s
            s0 = s0_ref[s, h]
            oi_ref[...] += jnp.where(mine, _bdot(qdh, s0), 0.0)
            b_seq = b_ref[pl.ds(pl.multiple_of(s * seq_len, seq_len), seq_len), hs]
            g_end = jnp.exp(b_seq[seq_len - 1:seq_len])
            new_t = s0.T * g_end + _bdot(jnp.where(mine, vh, 0.0), keh, _TN)
            st_ref[s, h] = new_t.T
            return carry

        lax.fori_loop(0, n_seq, per_seq, 0)
        yb_ref[:, hs] = _hgrn_out(oi_ref[...], gn[:, hs], og[:, hs])


def _mixer_specs(width, row_block, n_axes):
    def idx(f):
        return f if n_axes == 2 else (lambda t: f(t, 0))

    def zcol(j):
        return pl.BlockSpec((ROWS, width), idx(lambda a, c, j=j: (row_block(a, c), j)))

    def const(shape):
        return pl.BlockSpec(shape, idx(lambda a, c: (0,) * len(shape)))

    return zcol, const


def _mixer_prompt(layer, z, batch, width, lng, lnb, ws, bias, tri, lbl, gn):
    n_chunks = batch[1] // ROWS
    zcol, const = _mixer_specs(width, lambda a, c: a * n_chunks + c, 2)
    depth = lbl.shape[0]
    dk = width // HG_HEADS
    n_tok = batch[0] * batch[1]
    return pl.pallas_call(
        functools.partial(_mixer_prompt_kernel, layer),
        grid=(batch[0], n_chunks),
        in_specs=[zcol(j) for j in range(6)] + [
            const((1, width)), const((1, width)), const((SGU_GROUPS, ROWS, ROWS)),
            const((ROWS, width)), const((ROWS, ROWS)), const((depth, width)), const((1, width))],
        out_specs=[pl.BlockSpec((ROWS, width), lambda a, c: (a * n_chunks + c, 0)),
                   pl.BlockSpec((ROWS, width), lambda a, c: (a * n_chunks + c, 0)),
                   pl.BlockSpec((1, HG_HEADS, dk, dk), lambda a, c: (a, 0, 0, 0))],
        out_shape=[jax.ShapeDtypeStruct((n_tok, width), BF16),
                   jax.ShapeDtypeStruct((n_tok, width), BF16),
                   jax.ShapeDtypeStruct((batch[0], HG_HEADS, dk, dk), F32)],
        scratch_shapes=[pltpu.VMEM((HG_HEADS, dk, dk), F32), pltpu.VMEM((ROWS, width), F32)],
        compiler_params=_params(("parallel", "arbitrary"), 56),
        name="mixer_prompt",
    )(z, z, z, z, z, z, lng, lnb, ws, bias, tri, lbl, gn)


def _mixer_sample(layer, z, row0, batch, width, lng, lnb, ws, bias, tri, lbl, gn, s0):
    n_seq, seq_len = batch
    per_step = ROWS // seq_len
    steps = n_seq // per_step
    blk0 = row0 // ROWS
    zcol, const = _mixer_specs(width, lambda t, c: blk0 + t, 1)
    depth = lbl.shape[0]
    dk = width // HG_HEADS
    n_tok = n_seq * seq_len
    st_spec = pl.BlockSpec((per_step, HG_HEADS, dk, dk), lambda t: (t, 0, 0, 0))
    row_spec = pl.BlockSpec((ROWS, width), lambda t: (t, 0))
    return pl.pallas_call(
        functools.partial(_mixer_sample_kernel, layer, seq_len),
        grid=(steps,),
        in_specs=[zcol(j) for j in range(6)] + [
            const((1, width)), const((1, width)), const((SGU_GROUPS, ROWS, ROWS)),
            const((ROWS, width)), const((ROWS, ROWS)), const((depth, width)), const((1, width)),
            st_spec],
        out_specs=[row_spec, row_spec, row_spec, st_spec],
        out_shape=[jax.ShapeDtypeStruct((n_tok, width), BF16),
                   jax.ShapeDtypeStruct((n_tok, width), BF16),
                   jax.ShapeDtypeStruct((n_tok, width), F32),
                   jax.ShapeDtypeStruct((n_seq, HG_HEADS, dk, dk), F32)],
        scratch_shapes=[pltpu.VMEM((ROWS, width), F32), pltpu.VMEM((ROWS, dk), F32)],
        compiler_params=_params(("parallel",), 58),
        name="mixer_sample",
    )(z, z, z, z, z, z, lng, lnb, ws, bias, tri, lbl, gn, s0)


def _merge_kernel(alpha, ya_ref, yb_ref, ga_ref, gb_ref, x_ref, wpa_ref, wpb_ref, wo_ref, g_ref, b_ref,
                  x1_ref, x1t_ref):
    a = jnp.dot(ya_ref[...], wpa_ref[...], preferred_element_type=F32)
    b = jnp.dot(yb_ref[...], wpb_ref[...], preferred_element_type=F32)
    merged = _sigmoid(ga_ref[...]) * a + _sigmoid(gb_ref[...]) * b
    y = jnp.dot(merged.astype(BF16), wo_ref[...], preferred_element_type=F32)
    x1 = _layer_norm(alpha * x_ref[...] + y, g_ref[...], b_ref[...])
    x1_ref[...] = x1
    x1t_ref[...] = x1.T.astype(BF16)


def _merge(alpha, ya, yb, z, x, wpa, wpb, wo, g, b, tm, gate_blocks):
    n_tok, d = x.shape
    da = ya.shape[1]
    ga_blk, gb_blk = gate_blocks
    const = lambda shape: pl.BlockSpec(shape, lambda i: (0,) * len(shape), pipeline_mode=pl.Buffered(1))
    return pl.pallas_call(
        functools.partial(_merge_kernel, alpha),
        grid=(n_tok // tm,),
        in_specs=[pl.BlockSpec((tm, da), lambda i: (i, 0)),
                  pl.BlockSpec((tm, da), lambda i: (i, 0)),
                  pl.BlockSpec((tm, d), lambda i: (i, ga_blk)),
                  pl.BlockSpec((tm, d), lambda i: (i, gb_blk)),
                  pl.BlockSpec((tm, d), lambda i: (i, 0)),
                  const((da, d)), const((da, d)), const((d, d)), const((1, d)), const((1, d))],
        out_specs=[pl.BlockSpec((tm, d), lambda i: (i, 0)),
                   pl.BlockSpec((d, tm), lambda i: (0, i))],
        out_shape=[jax.ShapeDtypeStruct((n_tok, d), F32),
                   jax.ShapeDtypeStruct((d, n_tok), BF16)],
        compiler_params=_params(("parallel",), 58),
        name="merge_ln",
    )(ya, yb, z, z, x, wpa, wpb, wo, g, b)


def _top_values(a, count):
    out = []
    for _ in range(count):
        m = jnp.max(a, axis=0, keepdims=True)
        out.append(m)
        a = jnp.where(a == m, -jnp.inf, a)
    return out


def _stack_rows(rows, n):
    lanes = rows[0].shape[1]
    idx = lax.broadcasted_iota(jnp.int32, (n, lanes), 0)
    acc = jnp.broadcast_to(rows[0], (n, lanes))
    for r in range(1, n):
        acc = jnp.where(idx == r, rows[r], acc)
    return acc


def _peer_route_kernel(xt_ref, wqt_ref, sk_ref, thr_ref, c_ref, s2_ref, e2_ref, q_ref):
    q_ref[...] = jnp.dot(wqt_ref[...], xt_ref[...], preferred_element_type=F32).astype(BF16)
    half = sk_ref.shape[3]

    def per_head(h, carry):
        base = pl.multiple_of(h * 2 * half, 2 * half)
        s1 = jnp.dot(sk_ref[h, 0], q_ref[pl.ds(base, half), :], preferred_element_type=F32)
        s2 = jnp.dot(sk_ref[h, 1], q_ref[pl.ds(base + half, half), :], preferred_element_type=F32)
        t1 = _top_values(s1, PEER_TOPK)
        t2 = _top_values(s2, PEER_TOPK)
        t2_all = _stack_rows(t2, PEER_TOPK)
        t2_half = t2_all[:PEER_TOPK // 2]
        cand = jnp.concatenate([t1[0] + t2_all] + [t1[a] + t2_half for a in range(1, PEER_TOPK)], axis=0)
        top = _top_values(cand, PEER_TOPK)
        z = jnp.zeros_like(top[0])
        for t in top:
            z = z + jnp.exp(t - top[0])
        thr_ref[h] = top[-1] - s1
        c_ref[h] = jnp.exp(s1 - t1[0]) / z
        s2_ref[h] = s2
        e2_ref[h] = jnp.exp(s2 - t2[0])
        return carry

    lax.fori_loop(0, PEER_HEADS, per_head, 0)


def _peer_route(xt, wqt, sk, tb):
    d, n_tok = xt.shape
    nk = sk.shape[2]
    out = jax.ShapeDtypeStruct((PEER_HEADS, nk, n_tok), F32)
    ospec = pl.BlockSpec((PEER_HEADS, nk, tb), lambda i: (0, 0, i))
    return pl.pallas_call(
        _peer_route_kernel,
        grid=(n_tok // tb,),
        in_specs=[pl.BlockSpec((d, tb), lambda i: (0, i)),
                  pl.BlockSpec(wqt.shape, lambda i: (0, 0)),
                  pl.BlockSpec(sk.shape, lambda i: (0, 0, 0, 0))],
        out_specs=[ospec] * 4,
        out_shape=[out] * 4,
        scratch_shapes=[pltpu.VMEM((wqt.shape[0], tb), BF16)],
        compiler_params=_params(("parallel",), 48),
        name="peer_route",
    )(xt, wqt, sk)


def _peer_dense_kernel(lane_chunk, xt_ref, u_ref, vt_ref, thr_ref, c_ref, s2_ref, e2_ref, out_ref,
                       ht_ref, pt_ref):
    @pl.when(pl.program_id(1) == 0)
    def _():
        out_ref[...] = jnp.zeros_like(out_ref)

    ht_ref[...] = jnp.dot(u_ref[...], xt_ref[...], preferred_element_type=F32)
    nk = s2_ref.shape[1]
    tb = xt_ref.shape[1]

    def per_lane_chunk(tc, carry):
        ls = pl.ds(pl.multiple_of(tc * lane_chunk, lane_chunk), lane_chunk)
        for ii in range(u_ref.shape[0] // nk):
            rs = slice(ii * nk, (ii + 1) * nk)
            gate = jnp.zeros((nk, lane_chunk), F32)
            for h in range(PEER_HEADS):
                thr = thr_ref[h, ii:ii + 1, ls]
                cc = c_ref[h, ii:ii + 1, ls]
                gate = gate + jnp.where(s2_ref[h, :, ls] >= thr, e2_ref[h, :, ls], 0.0) * cc
            pt_ref[rs, ls] = (gate * _gelu(ht_ref[rs, ls])).astype(BF16)
        return carry

    lax.fori_loop(0, tb // lane_chunk, per_lane_chunk, 0)
    out_ref[...] += jnp.dot(vt_ref[...], pt_ref[...], preferred_element_type=F32)


def _peer_dense(xt, u, vt, thr, c, s2, e2, tb, eb):
    d, n_tok = xt.shape
    n_exp = u.shape[0]
    nk = s2.shape[1]
    keys_per_blk = eb // nk
    row_spec = pl.BlockSpec((PEER_HEADS, keys_per_blk, tb), lambda t, e: (0, e, t))
    all_spec = pl.BlockSpec((PEER_HEADS, nk, tb), lambda t, e: (0, 0, t))
    return pl.pallas_call(
        functools.partial(_peer_dense_kernel, 128),
        grid=(n_tok // tb, n_exp // eb),
        in_specs=[pl.BlockSpec((d, tb), lambda t, e: (0, t)),
                  pl.BlockSpec((eb, d), lambda t, e: (e, 0)),
                  pl.BlockSpec((d, eb), lambda t, e: (0, e)),
                  row_spec, row_spec, all_spec, all_spec],
        out_specs=pl.BlockSpec((d, tb), lambda t, e: (0, t)),
        out_shape=jax.ShapeDtypeStruct((d, n_tok), F32),
        scratch_shapes=[pltpu.VMEM((eb, tb), F32), pltpu.VMEM((eb, tb), BF16)],
        compiler_params=_params(("parallel", "arbitrary"), 56),
        name="peer_dense",
    )(xt, u, vt, thr, c, s2, e2)


def _residual_ln_kernel(alpha, x_ref, pt_ref, g_ref, b_ref, o_ref, ob_ref):
    y = _layer_norm(alpha * x_ref[...] + pt_ref[...].T, g_ref[...], b_ref[...])
    o_ref[...] = y
    ob_ref[...] = y.astype(BF16)


def _residual_ln(alpha, x, pt, g, b, tm):
    n_tok, d = x.shape
    return pl.pallas_call(
        functools.partial(_residual_ln_kernel, alpha),
        grid=(n_tok // tm,),
        in_specs=[pl.BlockSpec((tm, d), lambda i: (i, 0)),
                  pl.BlockSpec((d, tm), lambda i: (0, i)),
                  pl.BlockSpec((1, d), lambda i: (0, 0)),
                  pl.BlockSpec((1, d), lambda i: (0, 0))],
        out_specs=[pl.BlockSpec((tm, d), lambda i: (i, 0))] * 2,
        out_shape=[jax.ShapeDtypeStruct((n_tok, d), F32), jax.ShapeDtypeStruct((n_tok, d), BF16)],
        compiler_params=_params(("parallel",), 40),
        name="residual_ln",
    )(x, pt, g, b)


def _block_diag(w, reps):
    return jnp.kron(jnp.eye(reps, dtype=w.dtype), w)


def kernel(x_prompt, x_sample, state_hgrn, w_in, sgu_ln_g, sgu_ln_b, sgu_w, sgu_b, hg_lb_logits,
           hg_norm_g, w_branch_a, w_branch_b, w_out, ln1_g, ln1_b, peer_w_q, peer_sub_keys,
           peer_u, peer_v, ln2_g, ln2_b):
    depth = w_in.shape[0]
    n_b, seq, d = x_prompt.shape
    n_s, dec_seq, _ = x_sample.shape
    width = sgu_ln_g.shape[1]
    n_prompt = n_b * seq
    alpha = (2 * depth) ** 0.25
    assert seq % ROWS == 0 and ROWS % dec_seq == 0 and n_s % (ROWS // dec_seq) == 0
    assert w_in.shape[2] == 6 * width + 2 * d and d == 2 * width

    x = jnp.concatenate([x_prompt.reshape(n_prompt, d), x_sample.reshape(n_s * dec_seq, d)], axis=0)
    xb = x.astype(BF16)
    row = lambda a: a.reshape(1, -1)

    tri = jnp.tril(jnp.ones((ROWS, ROWS), F32))
    tri_s = _block_diag(jnp.tril(jnp.ones((dec_seq, dec_seq), F32)), ROWS // dec_seq)

    new_sp, new_ss, new_v = [], [], []
    for l in range(depth):
        z = _matmul(xb, w_in[l].astype(BF16), 512, 1024, "in_proj")

        ws = jnp.where(tri > 0, sgu_w[l], 0.0)
        ws_s = jax.vmap(lambda w: _block_diag(w, ROWS // dec_seq))(ws[:, :dec_seq, :dec_seq])
        bias = jnp.repeat(sgu_b[l].T, width // SGU_GROUPS, axis=1)
        bias_s = jnp.tile(bias[:dec_seq], (ROWS // dec_seq, 1))
        common = (row(sgu_ln_g[l]), row(sgu_ln_b[l]))
        tail = (hg_lb_logits, row(hg_norm_g[l]))
        ya_p, yb_p, st_p = _mixer_prompt(l, z, (n_b, seq), width, *common, ws.astype(BF16), bias,
                                         tri.astype(BF16), *tail)
        ya_s, yb_s, vn_s, st_s = _mixer_sample(l, z, n_prompt, (n_s, dec_seq), width, *common,
                                               ws_s.astype(BF16), bias_s, tri_s.astype(BF16), *tail,
                                               state_hgrn[l])
        ya = jnp.concatenate([ya_p, ya_s], axis=0)
        yb = jnp.concatenate([yb_p, yb_s], axis=0)

        x1, x1t = _merge(alpha, ya, yb, z, x, w_branch_a[l].astype(BF16), w_branch_b[l].astype(BF16),
                         w_out[l].astype(BF16), row(ln1_g[l]), row(ln1_b[l]), 256,
                         (6 * width // d, 6 * width // d + 1))

        thr, c, s2, e2 = _peer_route(x1t, peer_w_q[l].T.astype(BF16), peer_sub_keys[l].astype(BF16), 256)
        pt = _peer_dense(x1t, peer_u[l].astype(BF16), peer_v[l].T.astype(BF16), thr, c, s2, e2, 512, 1024)
        x, xb = _residual_ln(alpha, x1, pt, row(ln2_g[l]), row(ln2_b[l]), 512)

        new_sp.append(st_p)
        new_ss.append(st_s)
        new_v.append(vn_s.reshape(n_s, dec_seq, width))

    return (x[:n_prompt].reshape(n_b, seq, d), x[n_prompt:].reshape(n_s, dec_seq, d),
            jnp.stack(new_sp), jnp.stack(new_ss), jnp.stack(new_v))
```

```python
import functools

import jax
import jax.numpy as jnp
from jax import lax
from jax.experimental import pallas as pl
from jax.experimental.pallas import tpu as pltpu

F32 = jnp.float32
BF16 = jnp.bfloat16

SGU_GROUPS = 4
HG_HEADS = 8
PEER_HEADS = 8
PEER_TOPK = 16
LN_EPS = 1e-5
RMS_EPS = 1e-6
ROWS = 128
HG_SUB = 16
MIB = 1024 * 1024

_NT = (((1,), (1,)), ((), ()))
_TN = (((0,), (0,)), ((), ()))


def _params(sem, vmem_mib):
    return pltpu.CompilerParams(dimension_semantics=sem, vmem_limit_bytes=vmem_mib * MIB)


def _gelu(x):
    return 0.5 * x * (1.0 + jnp.tanh(0.7978845608028654 * (x + 0.044715 * (x * x * x))))


def _sigmoid(x):
    return 1.0 / (1.0 + jnp.exp(-x))


def _layer_norm(x, g, b):
    mu = jnp.mean(x, axis=-1, keepdims=True)
    xc = x - mu
    var = jnp.mean(xc * xc, axis=-1, keepdims=True)
    return xc * lax.rsqrt(var + LN_EPS) * g + b


def _bdot(a, b, dims=None):
    a = a.astype(BF16)
    b = b.astype(BF16)
    if dims is None:
        return jnp.dot(a, b, preferred_element_type=F32)
    return lax.dot_general(a, b, dims, preferred_element_type=F32)


def _matmul_kernel(x_ref, w_ref, o_ref):
    o_ref[...] = jnp.dot(x_ref[...], w_ref[...], preferred_element_type=F32)


def _matmul(x, w, tm, tn, name):
    m, k = x.shape
    n = w.shape[1]
    return pl.pallas_call(
        _matmul_kernel,
        grid=(n // tn, m // tm),
        in_specs=[pl.BlockSpec((tm, k), lambda j, i: (i, 0)),
                  pl.BlockSpec((k, tn), lambda j, i: (0, j))],
        out_specs=pl.BlockSpec((tm, tn), lambda j, i: (i, j)),
        out_shape=jax.ShapeDtypeStruct((m, n), F32),
        compiler_params=_params(("parallel", "parallel"), 40),
        name=name,
    )(x, w)


def _forget_lower_bound(logits_ref, layer):
    lg = logits_ref[...]
    e = jnp.exp(lg - jnp.max(lg, axis=0, keepdims=True))
    p = e / jnp.sum(e, axis=0, keepdims=True)
    return jnp.sum(p[1:layer + 1], axis=0, keepdims=True)


def _hgrn_gates(fp, lb):
    e = jnp.exp(-jnp.abs(fp))
    log_sig = jnp.minimum(fp, 0.0) - jnp.log1p(e)
    sig_neg = jnp.where(fp >= 0.0, e, 1.0) / (1.0 + e)
    if lb is None:
        return log_sig, sig_neg
    k = (1.0 - lb) * sig_neg
    pos = lb > 0.0
    a = jnp.log(jnp.where(pos, lb, 1.0))
    b = jnp.log1p(-lb) + log_sig
    lse = jnp.maximum(a, b) + jnp.log1p(jnp.exp(-jnp.abs(a - b)))
    return jnp.where(pos, lse, b), k


def _segment_cumsum(t_ref, x):
    t = t_ref[...]
    hi = x.astype(BF16)
    r1 = x - hi.astype(F32)
    mid = r1.astype(BF16)
    lo = (r1 - mid.astype(F32)).astype(BF16)
    dot = functools.partial(jnp.dot, preferred_element_type=F32)
    return dot(t, hi) + dot(t, mid) + dot(t, lo)


def _sgu(u, v, lng_ref, lnb_ref, ws_ref, bias_ref, ya_ref):
    vn = _layer_norm(_gelu(v), lng_ref[...], lnb_ref[...])
    vnb = vn.astype(BF16)
    gw = vn.shape[1] // SGU_GROUPS
    for g in range(SGU_GROUPS):
        cs = slice(g * gw, (g + 1) * gw)
        mixed = jnp.dot(ws_ref[g], vnb[:, cs], preferred_element_type=F32) + bias_ref[:, cs]
        ya_ref[:, cs] = (_gelu(u[:, cs]) * mixed).astype(BF16)
    return vn


def _hgrn_out(o, gn, og):
    o = o * lax.rsqrt(jnp.mean(o * o, axis=-1, keepdims=True) + RMS_EPS)
    return (o * gn * (og * _sigmoid(og))).astype(BF16)


def _rows_at(b_ref, size, offset, width):
    parts = []
    for j in range(ROWS // size):
        r = j * size + offset
        if r < 0:
            parts.append(jnp.zeros((size, width), F32))
        else:
            parts.append(jnp.broadcast_to(b_ref[pl.ds(r, 1), :], (size, width)))
    return parts[0] if len(parts) == 1 else jnp.concatenate(parts, axis=0)


def _mixer_prompt_kernel(layer, u_ref, v_ref, q_ref, f_ref, i_ref, og_ref, lng_ref, lnb_ref,
                         ws_ref, bias_ref, tri_ref, lbl_ref, gn_ref,
                         ya_ref, yb_ref, st_ref, stt_ref, b_ref):
    c = pl.program_id(1)
    width = q_ref.shape[1]
    dk = width // HG_HEADS

    @pl.when(c == 0)
    def _():
        stt_ref[...] = jnp.zeros_like(stt_ref)

    _sgu(u_ref[...], v_ref[...], lng_ref, lnb_ref, ws_ref, bias_ref, ya_ref)

    lb = None if layer == 0 else _forget_lower_bound(lbl_ref, layer)
    lf, k = _hgrn_gates(f_ref[...], lb)
    qs = q_ref[...] * (dk ** -0.5)
    b = _segment_cumsum(tri_ref, lf)
    b_ref[...] = b

    e_sub = b - _rows_at(b_ref, HG_SUB, -1, width)
    q_lvl = [qs * jnp.exp(e_sub)]
    k_lvl = [k * jnp.exp(-e_sub)]
    size = HG_SUB
    sizes = []
    while size < ROWS:
        sizes.append(size)
        q_lvl.append(q_lvl[0] if size == HG_SUB else qs * jnp.exp(b - _rows_at(b_ref, size, -1, width)))
        k_lvl.append(k * jnp.exp(_rows_at(b_ref, size, size - 1, width) - b))
        size *= 2
    qd = qs * jnp.exp(b)
    b_end = b_ref[pl.ds(ROWS - 1, 1), :]
    ke = k * jnp.exp(b_end - b)
    g_end = jnp.exp(b_end)

    row = lax.broadcasted_iota(jnp.int32, (ROWS, ROWS), 0)
    col = lax.broadcasted_iota(jnp.int32, (ROWS, ROWS), 1)
    masks = [(row // HG_SUB == col // HG_SUB) & (col <= row)]
    for sz in sizes:
        masks.append(((row // sz) % 2 == 1) & (col // sz == row // sz - 1))

    vv = i_ref[...]
    gn = gn_ref[...]
    og = og_ref[...]
    for h in range(HG_HEADS):
        hs = slice(h * dk, (h + 1) * dk)
        attn = jnp.zeros((ROWS, ROWS), F32)
        for ql, kl, mk in zip(q_lvl, k_lvl, masks):
            attn = jnp.where(mk, _bdot(ql[:, hs], kl[:, hs], _NT), attn)
        vh = vv[:, hs]
        stt = stt_ref[h]
        o = _bdot(attn, vh) + _bdot(qd[:, hs], stt, _NT)
        stt_ref[h] = stt * g_end[:, hs] + _bdot(vh, ke[:, hs], _TN)
        yb_ref[:, hs] = _hgrn_out(o, gn[:, hs], og[:, hs])

    @pl.when(c == pl.num_programs(1) - 1)
    def _():
        for h in range(HG_HEADS):
            st_ref[0, h] = stt_ref[h].T


def _mixer_sample_kernel(layer, seq_len, u_ref, v_ref, q_ref, f_ref, i_ref, og_ref, lng_ref, lnb_ref,
                         ws_ref, bias_ref, tri_ref, lbl_ref, gn_ref, s0_ref,
                         ya_ref, yb_ref, vn_ref, st_ref, b_ref, oi_ref):
    width = q_ref.shape[1]
    dk = width // HG_HEADS
    n_seq = ROWS // seq_len

    vn_ref[...] = _sgu(u_ref[...], v_ref[...], lng_ref, lnb_ref, ws_ref, bias_ref, ya_ref)

    lb = None if layer == 0 else _forget_lower_bound(lbl_ref, layer)
    lf, k = _hgrn_gates(f_ref[...], lb)
    qs = q_ref[...] * (dk ** -0.5)
    b = _segment_cumsum(tri_ref, lf)
    b_ref[...] = b
    qd = qs * jnp.exp(b)
    k0 = k * jnp.exp(-b)
    ke = k * jnp.exp(_rows_at(b_ref, seq_len, seq_len - 1, width) - b)

    row = lax.broadcasted_iota(jnp.int32, (ROWS, ROWS), 0)
    col = lax.broadcasted_iota(jnp.int32, (ROWS, ROWS), 1)
    causal = (row // seq_len == col // seq_len) & (col <= row)
    seq_of_row = lax.broadcasted_iota(jnp.int32, (ROWS, dk), 0) // seq_len

    vv = i_ref[...]
    gn = gn_ref[...]
    og = og_ref[...]
    for h in range(HG_HEADS):
        hs = slice(h * dk, (h + 1) * dk)
        attn = jnp.where(causal, _bdot(qd[:, hs], k0[:, hs], _NT), 0.0)
        vh = vv[:, hs]
        qdh = qd[:, hs].astype(BF16)
        keh = ke[:, hs].astype(BF16)
        oi_ref[...] = _bdot(attn, vh)

        def per_seq(s, carry, h=h, hs=hs, vh=vh, qdh=qdh, keh=keh):
            mine = seq_of_row == s
            s0 = s0_ref[s, h]
            oi_ref[...] += jnp.where(mine, _bdot(qdh, s0), 0.0)
            b_seq = b_ref[pl.ds(pl.multiple_of(s * seq_len, seq_len), seq_len), hs]
            g_end = jnp.exp(b_seq[seq_len - 1:seq_len])
            new_t = s0.T * g_end + _bdot(jnp.where(mine, vh, 0.0), keh, _TN)
            st_ref[s, h] = new_t.T
            return carry

        lax.fori_loop(0, n_seq, per_seq, 0)
        yb_ref[:, hs] = _hgrn_out(oi_ref[...], gn[:, hs], og[:, hs])


def _mixer_specs(width, row_block, n_axes):
    def idx(f):
        return f if n_axes == 2 else (lambda t: f(t, 0))

    def zcol(j):
        return pl.BlockSpec((ROWS, width), idx(lambda a, c, j=j: (row_block(a, c), j)))

    def const(shape):
        return pl.BlockSpec(shape, idx(lambda a, c: (0,) * len(shape)))

    return zcol, const


def _mixer_prompt(layer, z, batch, width, lng, lnb, ws, bias, tri, lbl, gn):
    n_chunks = batch[1] // ROWS
    zcol, const = _mixer_specs(width, lambda a, c: a * n_chunks + c, 2)
    depth = lbl.shape[0]
    dk = width // HG_HEADS
    n_tok = batch[0] * batch[1]
    return pl.pallas_call(
        functools.partial(_mixer_prompt_kernel, layer),
        grid=(batch[0], n_chunks),
        in_specs=[zcol(j) for j in range(6)] + [
            const((1, width)), const((1, width)), const((SGU_GROUPS, ROWS, ROWS)),
            const((ROWS, width)), const((ROWS, ROWS)), const((depth, width)), const((1, width))],
        out_specs=[pl.BlockSpec((ROWS, width), lambda a, c: (a * n_chunks + c, 0)),
                   pl.BlockSpec((ROWS, width), lambda a, c: (a * n_chunks + c, 0)),
                   pl.BlockSpec((1, HG_HEADS, dk, dk), lambda a, c: (a, 0, 0, 0))],
        out_shape=[jax.ShapeDtypeStruct((n_tok, width), BF16),
                   jax.ShapeDtypeStruct((n_tok, width), BF16),
                   jax.ShapeDtypeStruct((batch[0], HG_HEADS, dk, dk), F32)],
        scratch_shapes=[pltpu.VMEM((HG_HEADS, dk, dk), F32), pltpu.VMEM((ROWS, width), F32)],
        compiler_params=_params(("parallel", "arbitrary"), 56),
        name="mixer_prompt",
    )(z, z, z, z, z, z, lng, lnb, ws, bias, tri, lbl, gn)


def _mixer_sample(layer, z, row0, batch, width, lng, lnb, ws, bias, tri, lbl, gn, s0):
    n_seq, seq_len = batch
    per_step = ROWS // seq_len
    steps = n_seq // per_step
    blk0 = row0 // ROWS
    zcol, const = _mixer_specs(width, lambda t, c: blk0 + t, 1)
    depth = lbl.shape[0]
    dk = width // HG_HEADS
    n_tok = n_seq * seq_len
    st_spec = pl.BlockSpec((per_step, HG_HEADS, dk, dk), lambda t: (t, 0, 0, 0))
    row_spec = pl.BlockSpec((ROWS, width), lambda t: (t, 0))
    return pl.pallas_call(
        functools.partial(_mixer_sample_kernel, layer, seq_len),
        grid=(steps,),
        in_specs=[zcol(j) for j in range(6)] + [
            const((1, width)), const((1, width)), const((SGU_GROUPS, ROWS, ROWS)),
            const((ROWS, width)), const((ROWS, ROWS)), const((depth, width)), const((1, width)),
            st_spec],
        out_specs=[row_spec, row_spec, row_spec, st_spec],
        out_shape=[jax.ShapeDtypeStruct((n_tok, width), BF16),
                   jax.ShapeDtypeStruct((n_tok, width), BF16),
                   jax.ShapeDtypeStruct((n_tok, width), F32),
                   jax.ShapeDtypeStruct((n_seq, HG_HEADS, dk, dk), F32)],
        scratch_shapes=[pltpu.VMEM((ROWS, width), F32), pltpu.VMEM((ROWS, dk), F32)],
        compiler_params=_params(("parallel",), 58),
        name="mixer_sample",
    )(z, z, z, z, z, z, lng, lnb, ws, bias, tri, lbl, gn, s0)


def _merge_kernel(alpha, ya_ref, yb_ref, ga_ref, gb_ref, x_ref, wpa_ref, wpb_ref, wo_ref, g_ref, b_ref,
                  x1_ref, x1t_ref):
    a = jnp.dot(ya_ref[...], wpa_ref[...], preferred_element_type=F32)
    b = jnp.dot(yb_ref[...], wpb_ref[...], preferred_element_type=F32)
    merged = _sigmoid(ga_ref[...]) * a + _sigmoid(gb_ref[...]) * b
    y = jnp.dot(merged.astype(BF16), wo_ref[...], preferred_element_type=F32)
    x1 = _layer_norm(alpha * x_ref[...] + y, g_ref[...], b_ref[...])
    x1_ref[...] = x1
    x1t_ref[...] = x1.T.astype(BF16)


def _merge(alpha, ya, yb, z, x, wpa, wpb, wo, g, b, tm, gate_blocks):
    n_tok, d = x.shape
    da = ya.shape[1]
    ga_blk, gb_blk = gate_blocks
    const = lambda shape: pl.BlockSpec(shape, lambda i: (0,) * len(shape), pipeline_mode=pl.Buffered(1))
    return pl.pallas_call(
        functools.partial(_merge_kernel, alpha),
        grid=(n_tok // tm,),
        in_specs=[pl.BlockSpec((tm, da), lambda i: (i, 0)),
                  pl.BlockSpec((tm, da), lambda i: (i, 0)),
                  pl.BlockSpec((tm, d), lambda i: (i, ga_blk)),
                  pl.BlockSpec((tm, d), lambda i: (i, gb_blk)),
                  pl.BlockSpec((tm, d), lambda i: (i, 0)),
                  const((da, d)), const((da, d)), const((d, d)), const((1, d)), const((1, d))],
        out_specs=[pl.BlockSpec((tm, d), lambda i: (i, 0)),
                   pl.BlockSpec((d, tm), lambda i: (0, i))],
        out_shape=[jax.ShapeDtypeStruct((n_tok, d), F32),
                   jax.ShapeDtypeStruct((d, n_tok), BF16)],
        compiler_params=_params(("parallel",), 58),
        name="merge_ln",
    )(ya, yb, z, z, x, wpa, wpb, wo, g, b)


def _top_values(a, count):
    out = []
    for _ in range(count):
        m = jnp.max(a, axis=0, keepdims=True)
        out.append(m)
        a = jnp.where(a == m, -jnp.inf, a)
    return out


def _stack_rows(rows, n):
    lanes = rows[0].shape[1]
    idx = lax.broadcasted_iota(jnp.int32, (n, lanes), 0)
    acc = jnp.broadcast_to(rows[0], (n, lanes))
    for r in range(1, n):
        acc = jnp.where(idx == r, rows[r], acc)
    return acc


def _peer_route_kernel(xt_ref, wqt_ref, sk_ref, thr_ref, c_ref, s2_ref, e2_ref, q_ref):
    q_ref[...] = jnp.dot(wqt_ref[...], xt_ref[...], preferred_element_type=F32).astype(BF16)
    half = sk_ref.shape[3]

    def per_head(h, carry):
        base = pl.multiple_of(h * 2 * half, 2 * half)
        s1 = jnp.dot(sk_ref[h, 0], q_ref[pl.ds(base, half), :], preferred_element_type=F32)
        s2 = jnp.dot(sk_ref[h, 1], q_ref[pl.ds(base + half, half), :], preferred_element_type=F32)
        t1 = _top_values(s1, PEER_TOPK)
        t2 = _top_values(s2, PEER_TOPK)
        t2_all = _stack_rows(t2, PEER_TOPK)
        t2_half = t2_all[:PEER_TOPK // 2]
        cand = jnp.concatenate([t1[0] + t2_all] + [t1[a] + t2_half for a in range(1, PEER_TOPK)], axis=0)
        top = _top_values(cand, PEER_TOPK)
        z = jnp.zeros_like(top[0])
        for t in top:
            z = z + jnp.exp(t - top[0])
        thr_ref[h] = top[-1] - s1
        c_ref[h] = jnp.exp(s1 - t1[0]) / z
        s2_ref[h] = s2
        e2_ref[h] = jnp.exp(s2 - t2[0])
        return carry

    lax.fori_loop(0, PEER_HEADS, per_head, 0)


def _peer_route(xt, wqt, sk, tb):
    d, n_tok = xt.shape
    nk = sk.shape[2]
    out = jax.ShapeDtypeStruct((PEER_HEADS, nk, n_tok), F32)
    ospec = pl.BlockSpec((PEER_HEADS, nk, tb), lambda i: (0, 0, i))
    return pl.pallas_call(
        _peer_route_kernel,
        grid=(n_tok // tb,),
        in_specs=[pl.BlockSpec((d, tb), lambda i: (0, i)),
                  pl.BlockSpec(wqt.shape, lambda i: (0, 0)),
                  pl.BlockSpec(sk.shape, lambda i: (0, 0, 0, 0))],
        out_specs=[ospec] * 4,
        out_shape=[out] * 4,
        scratch_shapes=[pltpu.VMEM((wqt.shape[0], tb), BF16)],
        compiler_params=_params(("parallel",), 48),
        name="peer_route",
    )(xt, wqt, sk)


def _peer_dense_kernel(n_tiles, n_exp_tiles, lane_chunk, xt_ref, u_ref, vt_ref, thr_ref, c_ref,
                       s2_ref, e2_ref, out_ref, ht_a, ht_b, pt_a, pt_b):
    s = pl.program_id(0)
    retrieved = jnp.clip(s - 2, 0, n_tiles - 1)

    @pl.when(s == 0)
    def _():
        for ref in (ht_a, ht_b, pt_a, pt_b):
            ref[...] = jnp.zeros_like(ref)

    @pl.when(retrieved % n_exp_tiles == 0)
    def _():
        out_ref[...] = jnp.zeros_like(out_ref)

    nk = s2_ref.shape[1]
    tb = xt_ref.shape[1]
    n_keys = u_ref.shape[0] // nk
    sub = 16

    def stages(ht_new, ht_old, pt_new, pt_old):
        def project(rs, ls):
            ht_new[rs, ls] = jnp.dot(u_ref[rs, :], xt_ref[:, ls], preferred_element_type=F32)

        def gate(keys, j0, ls):
            acc = {}
            for h in range(PEER_HEADS):
                s2 = s2_ref[h, j0:j0 + sub, ls]
                e2 = e2_ref[h, j0:j0 + sub, ls]
                for ii in keys:
                    hit = s2 >= thr_ref[h, ii:ii + 1, ls]
                    term = jnp.where(hit, e2, 0.0) * c_ref[h, ii:ii + 1, ls]
                    acc[ii] = term if h == 0 else acc[ii] + term
            for ii in keys:
                rs = slice(ii * nk + j0, ii * nk + j0 + sub)
                pt_new[rs, ls] = (acc[ii] * _gelu(ht_old[rs, ls])).astype(BF16)

        def retrieve(rs, ls):
            out_ref[rs, ls] += jnp.dot(vt_ref[rs, :], pt_old[:, ls], preferred_element_type=F32)

        lanes = [slice(t0, t0 + lane_chunk) for t0 in range(0, tb, lane_chunk)]
        blk = 256
        p_units = [functools.partial(project, slice(r0, r0 + blk), ls)
                   for ls in lanes for r0 in range(0, u_ref.shape[0], blk)]
        r_units = [functools.partial(retrieve, slice(r0, r0 + blk), ls)
                   for ls in lanes for r0 in range(0, vt_ref.shape[0], blk)]
        key_groups = [range(n_keys)]
        g_units = [functools.partial(gate, keys, j0, slice(g0, g0 + 128))
                   for g0 in range(0, tb, 128) for j0 in range(0, nk, sub) for keys in key_groups]
        for n, g_unit in enumerate(g_units):
            for units in (p_units, r_units):
                if n * len(units) % len(g_units) == 0:
                    units[n * len(units) // len(g_units)]()
            g_unit()

    @pl.when(s % 2 == 0)
    def _():
        stages(ht_a, ht_b, pt_b, pt_a)

    @pl.when(s % 2 == 1)
    def _():
        stages(ht_b, ht_a, pt_a, pt_b)


def _peer_dense(xt, u, vt, thr, c, s2, e2, tb, eb):
    d, n_tok = xt.shape
    n_exp = u.shape[0]
    nk = s2.shape[1]
    keys_per_blk = eb // nk
    n_e = n_exp // eb
    n_tiles = (n_tok // tb) * n_e
    proj = lambda s: jnp.minimum(s, n_tiles - 1)
    gate = lambda s: jnp.clip(s - 1, 0, n_tiles - 1)
    retr = lambda s: jnp.clip(s - 2, 0, n_tiles - 1)
    row_spec = pl.BlockSpec((PEER_HEADS, keys_per_blk, tb), lambda s: (0, gate(s) % n_e, gate(s) // n_e))
    all_spec = pl.BlockSpec((PEER_HEADS, nk, tb), lambda s: (0, 0, gate(s) // n_e))
    return pl.pallas_call(
        functools.partial(_peer_dense_kernel, n_tiles, n_e, 256),
        grid=(n_tiles + 2,),
        in_specs=[pl.BlockSpec((d, tb), lambda s: (0, proj(s) // n_e)),
                  pl.BlockSpec((eb, d), lambda s: (proj(s) % n_e, 0)),
                  pl.BlockSpec((d, eb), lambda s: (0, retr(s) % n_e)),
                  row_spec, row_spec, all_spec, all_spec],
        out_specs=pl.BlockSpec((d, tb), lambda s: (0, retr(s) // n_e)),
        out_shape=jax.ShapeDtypeStruct((d, n_tok), F32),
        scratch_shapes=[pltpu.VMEM((eb, tb), F32), pltpu.VMEM((eb, tb), F32),
                        pltpu.VMEM((eb, tb), BF16), pltpu.VMEM((eb, tb), BF16)],
        compiler_params=_params(("arbitrary",), 58),
        name="peer_dense",
    )(xt, u, vt, thr, c, s2, e2)


def _residual_ln_kernel(alpha, x_ref, pt_ref, g_ref, b_ref, o_ref, ob_ref):
    y = _layer_norm(alpha * x_ref[...] + pt_ref[...].T, g_ref[...], b_ref[...])
    o_ref[...] = y
    ob_ref[...] = y.astype(BF16)


def _residual_ln(alpha, x, pt, g, b, tm):
    n_tok, d = x.shape
    return pl.pallas_call(
        functools.partial(_residual_ln_kernel, alpha),
        grid=(n_tok // tm,),
        in_specs=[pl.BlockSpec((tm, d), lambda i: (i, 0)),
                  pl.BlockSpec((d, tm), lambda i: (0, i)),
                  pl.BlockSpec((1, d), lambda i: (0, 0)),
                  pl.BlockSpec((1, d), lambda i: (0, 0))],
        out_specs=[pl.BlockSpec((tm, d), lambda i: (i, 0))] * 2,
        out_shape=[jax.ShapeDtypeStruct((n_tok, d), F32), jax.ShapeDtypeStruct((n_tok, d), BF16)],
        compiler_params=_params(("parallel",), 40),
        name="residual_ln",
    )(x, pt, g, b)


def _block_diag(w, reps):
    return jnp.kron(jnp.eye(reps, dtype=w.dtype), w)


def kernel(x_prompt, x_sample, state_hgrn, w_in, sgu_ln_g, sgu_ln_b, sgu_w, sgu_b, hg_lb_logits,
           hg_norm_g, w_branch_a, w_branch_b, w_out, ln1_g, ln1_b, peer_w_q, peer_sub_keys,
           peer_u, peer_v, ln2_g, ln2_b):
    depth = w_in.shape[0]
    n_b, seq, d = x_prompt.shape
    n_s, dec_seq, _ = x_sample.shape
    width = sgu_ln_g.shape[1]
    n_prompt = n_b * seq
    alpha = (2 * depth) ** 0.25
    assert seq % ROWS == 0 and ROWS % dec_seq == 0 and n_s % (ROWS // dec_seq) == 0
    assert w_in.shape[2] == 6 * width + 2 * d and d == 2 * width

    x = jnp.concatenate([x_prompt.reshape(n_prompt, d), x_sample.reshape(n_s * dec_seq, d)], axis=0)
    xb = x.astype(BF16)
    row = lambda a: a.reshape(1, -1)

    tri = jnp.tril(jnp.ones((ROWS, ROWS), F32))
    tri_s = _block_diag(jnp.tril(jnp.ones((dec_seq, dec_seq), F32)), ROWS // dec_seq)

    new_sp, new_ss, new_v = [], [], []
    for l in range(depth):
        z = _matmul(xb, w_in[l].astype(BF16), 512, 1024, "in_proj")

        ws = jnp.where(tri > 0, sgu_w[l], 0.0)
        ws_s = jax.vmap(lambda w: _block_diag(w, ROWS // dec_seq))(ws[:, :dec_seq, :dec_seq])
        bias = jnp.repeat(sgu_b[l].T, width // SGU_GROUPS, axis=1)
        bias_s = jnp.tile(bias[:dec_seq], (ROWS // dec_seq, 1))
        common = (row(sgu_ln_g[l]), row(sgu_ln_b[l]))
        tail = (hg_lb_logits, row(hg_norm_g[l]))
        ya_p, yb_p, st_p = _mixer_prompt(l, z, (n_b, seq), width, *common, ws.astype(BF16), bias,
                                         tri.astype(BF16), *tail)
        ya_s, yb_s, vn_s, st_s = _mixer_sample(l, z, n_prompt, (n_s, dec_seq), width, *common,
                                               ws_s.astype(BF16), bias_s, tri_s.astype(BF16), *tail,
                                               state_hgrn[l])
        ya = jnp.concatenate([ya_p, ya_s], axis=0)
        yb = jnp.concatenate([yb_p, yb_s], axis=0)

        x1, x1t = _merge(alpha, ya, yb, z, x, w_branch_a[l].astype(BF16), w_branch_b[l].astype(BF16),
                         w_out[l].astype(BF16), row(ln1_g[l]), row(ln1_b[l]), 256,
                         (6 * width // d, 6 * width // d + 1))

        thr, c, s2, e2 = _peer_route(x1t, peer_w_q[l].T.astype(BF16), peer_sub_keys[l].astype(BF16), 256)
        pt = _peer_dense(x1t, peer_u[l].astype(BF16), peer_v[l].T.astype(BF16), thr, c, s2, e2, 512, 1024)
        x, xb = _residual_ln(alpha, x1, pt, row(ln2_g[l]), row(ln2_b[l]), 512)

        new_sp.append(st_p)
        new_ss.append(st_s)
        new_v.append(vn_s.reshape(n_s, dec_seq, width))

    return (x[:n_prompt].reshape(n_b, seq, d), x[n_prompt:].reshape(n_s, dec_seq, d),
            jnp.stack(new_sp), jnp.stack(new_ss), jnp.stack(new_v))
```

```python
import functools

import jax
import jax.numpy as jnp
from jax import lax
from jax.experimental import pallas as pl
from jax.experimental.pallas import tpu as pltpu

F32 = jnp.float32
BF16 = jnp.bfloat16

SGU_GROUPS = 4
HG_HEADS = 8
PEER_HEADS = 8
PEER_TOPK = 16
LN_EPS = 1e-5
RMS_EPS = 1e-6
ROWS = 128
HG_SUB = 16
MIB = 1024 * 1024

_NT = (((1,), (1,)), ((), ()))
_TN = (((0,), (0,)), ((), ()))


def _params(sem, vmem_mib):
    return pltpu.CompilerParams(dimension_semantics=sem, vmem_limit_bytes=vmem_mib * MIB)


def _gelu(x):
    return 0.5 * x * (1.0 + jnp.tanh(0.7978845608028654 * (x + 0.044715 * (x * x * x))))


def _sigmoid(x):
    return 1.0 / (1.0 + jnp.exp(-x))


def _layer_norm(x, g, b):
    mu = jnp.mean(x, axis=-1, keepdims=True)
    xc = x - mu
    var = jnp.mean(xc * xc, axis=-1, keepdims=True)
    return xc * lax.rsqrt(var + LN_EPS) * g + b


def _bdot(a, b, dims=None):
    a = a.astype(BF16)
    b = b.astype(BF16)
    if dims is None:
        return jnp.dot(a, b, preferred_element_type=F32)
    return lax.dot_general(a, b, dims, preferred_element_type=F32)


def _matmul_kernel(x_ref, w_ref, o_ref):
    o_ref[...] = jnp.dot(x_ref[...], w_ref[...], preferred_element_type=F32)


def _matmul(x, w, tm, tn, name):
    m, k = x.shape
    n = w.shape[1]
    return pl.pallas_call(
        _matmul_kernel,
        grid=(n // tn, m // tm),
        in_specs=[pl.BlockSpec((tm, k), lambda j, i: (i, 0)),
                  pl.BlockSpec((k, tn), lambda j, i: (0, j))],
        out_specs=pl.BlockSpec((tm, tn), lambda j, i: (i, j)),
        out_shape=jax.ShapeDtypeStruct((m, n), F32),
        compiler_params=_params(("parallel", "parallel"), 40),
        name=name,
    )(x, w)


def _forget_lower_bound(logits_ref, layer):
    lg = logits_ref[...]
    e = jnp.exp(lg - jnp.max(lg, axis=0, keepdims=True))
    p = e / jnp.sum(e, axis=0, keepdims=True)
    return jnp.sum(p[1:layer + 1], axis=0, keepdims=True)


def _hgrn_gates(fp, lb):
    e = jnp.exp(-jnp.abs(fp))
    log_sig = jnp.minimum(fp, 0.0) - jnp.log1p(e)
    sig_neg = jnp.where(fp >= 0.0, e, 1.0) / (1.0 + e)
    if lb is None:
        return log_sig, sig_neg
    k = (1.0 - lb) * sig_neg
    pos = lb > 0.0
    a = jnp.log(jnp.where(pos, lb, 1.0))
    b = jnp.log1p(-lb) + log_sig
    lse = jnp.maximum(a, b) + jnp.log1p(jnp.exp(-jnp.abs(a - b)))
    return jnp.where(pos, lse, b), k


def _segment_cumsum(t_ref, x):
    t = t_ref[...]
    hi = x.astype(BF16)
    r1 = x - hi.astype(F32)
    mid = r1.astype(BF16)
    lo = (r1 - mid.astype(F32)).astype(BF16)
    dot = functools.partial(jnp.dot, preferred_element_type=F32)
    return dot(t, hi) + dot(t, mid) + dot(t, lo)


def _sgu(u, v, lng_ref, lnb_ref, ws_ref, bias_ref, ya_ref):
    vn = _layer_norm(_gelu(v), lng_ref[...], lnb_ref[...])
    vnb = vn.astype(BF16)
    gw = vn.shape[1] // SGU_GROUPS
    for g in range(SGU_GROUPS):
        cs = slice(g * gw, (g + 1) * gw)
        mixed = jnp.dot(ws_ref[g], vnb[:, cs], preferred_element_type=F32) + bias_ref[:, cs]
        ya_ref[:, cs] = (_gelu(u[:, cs]) * mixed).astype(BF16)
    return vn


def _hgrn_out(o, gn, og):
    o = o * lax.rsqrt(jnp.mean(o * o, axis=-1, keepdims=True) + RMS_EPS)
    return (o * gn * (og * _sigmoid(og))).astype(BF16)


def _rows_at(b_ref, size, offset, width):
    parts = []
    for j in range(ROWS // size):
        r = j * size + offset
        if r < 0:
            parts.append(jnp.zeros((size, width), F32))
        else:
            parts.append(jnp.broadcast_to(b_ref[pl.ds(r, 1), :], (size, width)))
    return parts[0] if len(parts) == 1 else jnp.concatenate(parts, axis=0)


def _mixer_prompt_kernel(layer, u_ref, v_ref, q_ref, f_ref, i_ref, og_ref, lng_ref, lnb_ref,
                         ws_ref, bias_ref, tri_ref, lbl_ref, gn_ref,
                         ya_ref, yb_ref, st_ref, stt_ref, b_ref):
    c = pl.program_id(1)
    width = q_ref.shape[1]
    dk = width // HG_HEADS

    @pl.when(c == 0)
    def _():
        stt_ref[...] = jnp.zeros_like(stt_ref)

    _sgu(u_ref[...], v_ref[...], lng_ref, lnb_ref, ws_ref, bias_ref, ya_ref)

    lb = None if layer == 0 else _forget_lower_bound(lbl_ref, layer)
    lf, k = _hgrn_gates(f_ref[...], lb)
    qs = q_ref[...] * (dk ** -0.5)
    b = _segment_cumsum(tri_ref, lf)
    b_ref[...] = b

    e_sub = b - _rows_at(b_ref, HG_SUB, -1, width)
    q_lvl = [qs * jnp.exp(e_sub)]
    k_lvl = [k * jnp.exp(-e_sub)]
    size = HG_SUB
    sizes = []
    while size < ROWS:
        sizes.append(size)
        q_lvl.append(q_lvl[0] if size == HG_SUB else qs * jnp.exp(b - _rows_at(b_ref, size, -1, width)))
        k_lvl.append(k * jnp.exp(_rows_at(b_ref, size, size - 1, width) - b))
        size *= 2
    qd = qs * jnp.exp(b)
    b_end = b_ref[pl.ds(ROWS - 1, 1), :]
    ke = k * jnp.exp(b_end - b)
    g_end = jnp.exp(b_end)

    row = lax.broadcasted_iota(jnp.int32, (ROWS, ROWS), 0)
    col = lax.broadcasted_iota(jnp.int32, (ROWS, ROWS), 1)
    masks = [(row // HG_SUB == col // HG_SUB) & (col <= row)]
    for sz in sizes:
        masks.append(((row // sz) % 2 == 1) & (col // sz == row // sz - 1))

    vv = i_ref[...]
    gn = gn_ref[...]
    og = og_ref[...]
    for h in range(HG_HEADS):
        hs = slice(h * dk, (h + 1) * dk)
        attn = jnp.zeros((ROWS, ROWS), F32)
        for ql, kl, mk in zip(q_lvl, k_lvl, masks):
            attn = jnp.where(mk, _bdot(ql[:, hs], kl[:, hs], _NT), attn)
        vh = vv[:, hs]
        stt = stt_ref[h]
        o = _bdot(attn, vh) + _bdot(qd[:, hs], stt, _NT)
        stt_ref[h] = stt * g_end[:, hs] + _bdot(vh, ke[:, hs], _TN)
        yb_ref[:, hs] = _hgrn_out(o, gn[:, hs], og[:, hs])

    @pl.when(c == pl.num_programs(1) - 1)
    def _():
        for h in range(HG_HEADS):
            st_ref[0, h] = stt_ref[h].T


def _mixer_sample_kernel(layer, seq_len, u_ref, v_ref, q_ref, f_ref, i_ref, og_ref, lng_ref, lnb_ref,
                         ws_ref, bias_ref, tri_ref, lbl_ref, gn_ref, s0_ref,
                         ya_ref, yb_ref, vn_ref, st_ref, b_ref, oi_ref):
    width = q_ref.shape[1]
    dk = width // HG_HEADS
    n_seq = ROWS // seq_len

    vn_ref[...] = _sgu(u_ref[...], v_ref[...], lng_ref, lnb_ref, ws_ref, bias_ref, ya_ref)

    lb = None if layer == 0 else _forget_lower_bound(lbl_ref, layer)
    lf, k = _hgrn_gates(f_ref[...], lb)
    qs = q_ref[...] * (dk ** -0.5)
    b = _segment_cumsum(tri_ref, lf)
    b_ref[...] = b
    qd = qs * jnp.exp(b)
    k0 = k * jnp.exp(-b)
    ke = k * jnp.exp(_rows_at(b_ref, seq_len, seq_len - 1, width) - b)

    row = lax.broadcasted_iota(jnp.int32, (ROWS, ROWS), 0)
    col = lax.broadcasted_iota(jnp.int32, (ROWS, ROWS), 1)
    causal = (row // seq_len == col // seq_len) & (col <= row)
    seq_of_row = lax.broadcasted_iota(jnp.int32, (ROWS, dk), 0) // seq_len

    vv = i_ref[...]
    gn = gn_ref[...]
    og = og_ref[...]
    for h in range(HG_HEADS):
        hs = slice(h * dk, (h + 1) * dk)
        attn = jnp.where(causal, _bdot(qd[:, hs], k0[:, hs], _NT), 0.0)
        vh = vv[:, hs]
        qdh = qd[:, hs].astype(BF16)
        keh = ke[:, hs].astype(BF16)
        oi_ref[...] = _bdot(attn, vh)

        def per_seq(s, carry, h=h, hs=hs, vh=vh, qdh=qdh, keh=keh):
            mine = seq_of_row == s
            s0 = s0_ref[s, h]
            oi_ref[...] += jnp.where(mine, _bdot(qdh, s0), 0.0)
            b_seq = b_ref[pl.ds(pl.multiple_of(s * seq_len, seq_len), seq_len), hs]
            g_end = jnp.exp(b_seq[seq_len - 1:seq_len])
            new_t = s0.T * g_end + _bdot(jnp.where(mine, vh, 0.0), keh, _TN)
            st_ref[s, h] = new_t.T
            return carry

        lax.fori_loop(0, n_seq, per_seq, 0)
        yb_ref[:, hs] = _hgrn_out(oi_ref[...], gn[:, hs], og[:, hs])


def _mixer_specs(width, row_block, n_axes):
    def idx(f):
        return f if n_axes == 2 else (lambda t: f(t, 0))

    def zcol(j):
        return pl.BlockSpec((ROWS, width), idx(lambda a, c, j=j: (row_block(a, c), j)))

    def const(shape):
        return pl.BlockSpec(shape, idx(lambda a, c: (0,) * len(shape)))

    return zcol, const


def _mixer_prompt(layer, z, batch, width, lng, lnb, ws, bias, tri, lbl, gn):
    n_chunks = batch[1] // ROWS
    zcol, const = _mixer_specs(width, lambda a, c: a * n_chunks + c, 2)
    depth = lbl.shape[0]
    dk = width // HG_HEADS
    n_tok = batch[0] * batch[1]
    return pl.pallas_call(
        functools.partial(_mixer_prompt_kernel, layer),
        grid=(batch[0], n_chunks),
        in_specs=[zcol(j) for j in range(6)] + [
            const((1, width)), const((1, width)), const((SGU_GROUPS, ROWS, ROWS)),
            const((ROWS, width)), const((ROWS, ROWS)), const((depth, width)), const((1, width))],
        out_specs=[pl.BlockSpec((ROWS, width), lambda a, c: (a * n_chunks + c, 0)),
                   pl.BlockSpec((ROWS, width), lambda a, c: (a * n_chunks + c, 0)),
                   pl.BlockSpec((1, HG_HEADS, dk, dk), lambda a, c: (a, 0, 0, 0))],
        out_shape=[jax.ShapeDtypeStruct((n_tok, width), BF16),
                   jax.ShapeDtypeStruct((n_tok, width), BF16),
                   jax.ShapeDtypeStruct((batch[0], HG_HEADS, dk, dk), F32)],
        scratch_shapes=[pltpu.VMEM((HG_HEADS, dk, dk), F32), pltpu.VMEM((ROWS, width), F32)],
        compiler_params=_params(("parallel", "arbitrary"), 56),
        name="mixer_prompt",
    )(z, z, z, z, z, z, lng, lnb, ws, bias, tri, lbl, gn)


def _mixer_sample(layer, z, row0, batch, width, lng, lnb, ws, bias, tri, lbl, gn, s0):
    n_seq, seq_len = batch
    per_step = ROWS // seq_len
    steps = n_seq // per_step
    blk0 = row0 // ROWS
    zcol, const = _mixer_specs(width, lambda t, c: blk0 + t, 1)
    depth = lbl.shape[0]
    dk = width // HG_HEADS
    n_tok = n_seq * seq_len
    st_spec = pl.BlockSpec((per_step, HG_HEADS, dk, dk), lambda t: (t, 0, 0, 0))
    row_spec = pl.BlockSpec((ROWS, width), lambda t: (t, 0))
    return pl.pallas_call(
        functools.partial(_mixer_sample_kernel, layer, seq_len),
        grid=(steps,),
        in_specs=[zcol(j) for j in range(6)] + [
            const((1, width)), const((1, width)), const((SGU_GROUPS, ROWS, ROWS)),
            const((ROWS, width)), const((ROWS, ROWS)), const((depth, width)), const((1, width)),
            st_spec],
        out_specs=[row_spec, row_spec, row_spec, st_spec],
        out_shape=[jax.ShapeDtypeStruct((n_tok, width), BF16),
                   jax.ShapeDtypeStruct((n_tok, width), BF16),
                   jax.ShapeDtypeStruct((n_tok, width), F32),
                   jax.ShapeDtypeStruct((n_seq, HG_HEADS, dk, dk), F32)],
        scratch_shapes=[pltpu.VMEM((ROWS, width), F32), pltpu.VMEM((ROWS, dk), F32)],
        compiler_params=_params(("parallel",), 58),
        name="mixer_sample",
    )(z, z, z, z, z, z, lng, lnb, ws, bias, tri, lbl, gn, s0)


def _merge_kernel(alpha, ya_ref, yb_ref, ga_ref, gb_ref, x_ref, wpa_ref, wpb_ref, wo_ref, g_ref, b_ref,
                  x1_ref, x1t_ref):
    a = jnp.dot(ya_ref[...], wpa_ref[...], preferred_element_type=F32)
    b = jnp.dot(yb_ref[...], wpb_ref[...], preferred_element_type=F32)
    merged = _sigmoid(ga_ref[...]) * a + _sigmoid(gb_ref[...]) * b
    y = jnp.dot(merged.astype(BF16), wo_ref[...], preferred_element_type=F32)
    x1 = _layer_norm(alpha * x_ref[...] + y, g_ref[...], b_ref[...])
    x1_ref[...] = x1
    x1t_ref[...] = x1.T.astype(BF16)


def _merge(alpha, ya, yb, z, x, wpa, wpb, wo, g, b, tm, gate_blocks):
    n_tok, d = x.shape
    da = ya.shape[1]
    ga_blk, gb_blk = gate_blocks
    const = lambda shape: pl.BlockSpec(shape, lambda i: (0,) * len(shape), pipeline_mode=pl.Buffered(1))
    return pl.pallas_call(
        functools.partial(_merge_kernel, alpha),
        grid=(n_tok // tm,),
        in_specs=[pl.BlockSpec((tm, da), lambda i: (i, 0)),
                  pl.BlockSpec((tm, da), lambda i: (i, 0)),
                  pl.BlockSpec((tm, d), lambda i: (i, ga_blk)),
                  pl.BlockSpec((tm, d), lambda i: (i, gb_blk)),
                  pl.BlockSpec((tm, d), lambda i: (i, 0)),
                  const((da, d)), const((da, d)), const((d, d)), const((1, d)), const((1, d))],
        out_specs=[pl.BlockSpec((tm, d), lambda i: (i, 0)),
                   pl.BlockSpec((d, tm), lambda i: (0, i))],
        out_shape=[jax.ShapeDtypeStruct((n_tok, d), F32),
                   jax.ShapeDtypeStruct((d, n_tok), BF16)],
        compiler_params=_params(("parallel",), 58),
        name="merge_ln",
    )(ya, yb, z, z, x, wpa, wpb, wo, g, b)


def _top_values(a, count):
    out = []
    for _ in range(count):
        m = jnp.max(a, axis=0, keepdims=True)
        out.append(m)
        a = jnp.where(a == m, -jnp.inf, a)
    return out


def _stack_rows(rows, n):
    lanes = rows[0].shape[1]
    idx = lax.broadcasted_iota(jnp.int32, (n, lanes), 0)
    acc = jnp.broadcast_to(rows[0], (n, lanes))
    for r in range(1, n):
        acc = jnp.where(idx == r, rows[r], acc)
    return acc


def _peer_route_kernel(xt_ref, wqt_ref, sk_ref, thr_ref, c_ref, s2_ref, e2_ref, q_ref):
    q_ref[...] = jnp.dot(wqt_ref[...], xt_ref[...], preferred_element_type=F32).astype(BF16)
    half = sk_ref.shape[3]

    def per_head(h, carry):
        base = pl.multiple_of(h * 2 * half, 2 * half)
        s1 = jnp.dot(sk_ref[h, 0], q_ref[pl.ds(base, half), :], preferred_element_type=F32)
        s2 = jnp.dot(sk_ref[h, 1], q_ref[pl.ds(base + half, half), :], preferred_element_type=F32)
        t1 = _top_values(s1, PEER_TOPK)
        t2 = _top_values(s2, PEER_TOPK)
        t2_all = _stack_rows(t2, PEER_TOPK)
        t2_half = t2_all[:PEER_TOPK // 2]
        cand = jnp.concatenate([t1[0] + t2_all] + [t1[a] + t2_half for a in range(1, PEER_TOPK)], axis=0)
        above = jnp.zeros_like(t1[0])
        tau = t1[0] + t2[0]
        for _ in range(PEER_TOPK):
            m = jnp.max(cand, axis=0, keepdims=True)
            eq = cand == m
            tau = jnp.where(above < PEER_TOPK, m, tau)
            above = above + jnp.sum(jnp.where(eq, 1.0, 0.0), axis=0, keepdims=True)
            cand = jnp.where(eq, -jnp.inf, cand)
        rows = [t1[a] + t2_all for a in range(PEER_TOPK)]
        count = lambda mask: jnp.sum(jnp.where(mask, 1.0, 0.0), axis=0, keepdims=True)
        n_gt = [count(r > tau) for r in rows]
        n_eq = [count(r == tau) for r in rows]
        spare = float(PEER_TOPK) - sum(n_gt)
        rank = lax.broadcasted_iota(jnp.int32, t2_all.shape, 0).astype(F32)
        top_score = t1[0] + t2[0]
        z = jnp.zeros_like(tau)
        thr = jnp.full(s1.shape, jnp.inf, F32)
        for a in range(PEER_TOPK):
            take = jnp.minimum(n_eq[a], spare)
            spare = spare - take
            keep = rank < n_gt[a] + take
            z = z + jnp.sum(jnp.where(keep, jnp.exp(rows[a] - top_score), 0.0), axis=0, keepdims=True)
            lowest_kept = jnp.min(jnp.where(keep, t2_all, jnp.inf), axis=0, keepdims=True)
            thr = jnp.where(s1 == t1[a], lowest_kept, thr)
        thr_ref[h] = thr
        c_ref[h] = jnp.exp(s1 - t1[0]) / z
        s2_ref[h] = s2
        e2_ref[h] = jnp.exp(s2 - t2[0])
        return carry

    lax.fori_loop(0, PEER_HEADS, per_head, 0)


def _peer_route(xt, wqt, sk, tb):
    d, n_tok = xt.shape
    nk = sk.shape[2]
    out = jax.ShapeDtypeStruct((PEER_HEADS, nk, n_tok), F32)
    ospec = pl.BlockSpec((PEER_HEADS, nk, tb), lambda i: (0, 0, i))
    return pl.pallas_call(
        _peer_route_kernel,
        grid=(n_tok // tb,),
        in_specs=[pl.BlockSpec((d, tb), lambda i: (0, i)),
                  pl.BlockSpec(wqt.shape, lambda i: (0, 0)),
                  pl.BlockSpec(sk.shape, lambda i: (0, 0, 0, 0))],
        out_specs=[ospec] * 4,
        out_shape=[out] * 4,
        scratch_shapes=[pltpu.VMEM((wqt.shape[0], tb), BF16)],
        compiler_params=_params(("parallel",), 48),
        name="peer_route",
    )(xt, wqt, sk)


def _peer_dense_kernel(n_tiles, n_exp_tiles, lane_chunk, xt_ref, u_ref, vt_ref, thr_ref, c_ref,
                       s2_ref, e2_ref, out_ref, ht_a, ht_b, pt_a, pt_b):
    s = pl.program_id(0)
    retrieved = jnp.clip(s - 2, 0, n_tiles - 1)

    @pl.when(s == 0)
    def _():
        for ref in (ht_a, ht_b, pt_a, pt_b):
            ref[...] = jnp.zeros_like(ref)

    @pl.when(retrieved % n_exp_tiles == 0)
    def _():
        out_ref[...] = jnp.zeros_like(out_ref)

    nk = s2_ref.shape[1]
    tb = xt_ref.shape[1]
    n_keys = u_ref.shape[0] // nk
    sub = 16

    def stages(ht_new, ht_old, pt_new, pt_old):
        def project(rs, ls):
            ht_new[rs, ls] = jnp.dot(u_ref[rs, :], xt_ref[:, ls], preferred_element_type=F32)

        def gate(keys, j0, ls):
            acc = {}
            for h in range(PEER_HEADS):
                s2 = s2_ref[h, j0:j0 + sub, ls]
                e2 = e2_ref[h, j0:j0 + sub, ls]
                for ii in keys:
                    hit = s2 >= thr_ref[h, ii:ii + 1, ls]
                    term = jnp.where(hit, e2, 0.0) * c_ref[h, ii:ii + 1, ls]
                    acc[ii] = term if h == 0 else acc[ii] + term
            for ii in keys:
                rs = slice(ii * nk + j0, ii * nk + j0 + sub)
                pt_new[rs, ls] = (acc[ii] * _gelu(ht_old[rs, ls])).astype(BF16)

        def retrieve(rs, ls):
            out_ref[rs, ls] += jnp.dot(vt_ref[rs, :], pt_old[:, ls], preferred_element_type=F32)

        lanes = [slice(t0, t0 + lane_chunk) for t0 in range(0, tb, lane_chunk)]
        blk = 256
        p_units = [functools.partial(project, slice(r0, r0 + blk), ls)
                   for ls in lanes for r0 in range(0, u_ref.shape[0], blk)]
        r_units = [functools.partial(retrieve, slice(r0, r0 + blk), ls)
                   for ls in lanes for r0 in range(0, vt_ref.shape[0], blk)]
        key_groups = [range(n_keys)]
        g_units = [functools.partial(gate, keys, j0, slice(g0, g0 + 128))
                   for g0 in range(0, tb, 128) for j0 in range(0, nk, sub) for keys in key_groups]
        for n, g_unit in enumerate(g_units):
            for units in (p_units, r_units):
                if n * len(units) % len(g_units) == 0:
                    units[n * len(units) // len(g_units)]()
            g_unit()

    @pl.when(s % 2 == 0)
    def _():
        stages(ht_a, ht_b, pt_b, pt_a)

    @pl.when(s % 2 == 1)
    def _():
        stages(ht_b, ht_a, pt_a, pt_b)


def _peer_dense(xt, u, vt, thr, c, s2, e2, tb, eb):
    d, n_tok = xt.shape
    n_exp = u.shape[0]
    nk = s2.shape[1]
    keys_per_blk = eb // nk
    n_e = n_exp // eb
    n_tiles = (n_tok // tb) * n_e
    proj = lambda s: jnp.minimum(s, n_tiles - 1)
    gate = lambda s: jnp.clip(s - 1, 0, n_tiles - 1)
    retr = lambda s: jnp.clip(s - 2, 0, n_tiles - 1)
    row_spec = pl.BlockSpec((PEER_HEADS, keys_per_blk, tb), lambda s: (0, gate(s) % n_e, gate(s) // n_e))
    all_spec = pl.BlockSpec((PEER_HEADS, nk, tb), lambda s: (0, 0, gate(s) // n_e))
    return pl.pallas_call(
        functools.partial(_peer_dense_kernel, n_tiles, n_e, 256),
        grid=(n_tiles + 2,),
        in_specs=[pl.BlockSpec((d, tb), lambda s: (0, proj(s) // n_e)),
                  pl.BlockSpec((eb, d), lambda s: (proj(s) % n_e, 0)),
                  pl.BlockSpec((d, eb), lambda s: (0, retr(s) % n_e)),
                  row_spec, row_spec, all_spec, all_spec],
        out_specs=pl.BlockSpec((d, tb), lambda s: (0, retr(s) // n_e)),
        out_shape=jax.ShapeDtypeStruct((d, n_tok), F32),
        scratch_shapes=[pltpu.VMEM((eb, tb), F32), pltpu.VMEM((eb, tb), F32),
                        pltpu.VMEM((eb, tb), BF16), pltpu.VMEM((eb, tb), BF16)],
        compiler_params=_params(("arbitrary",), 58),
        name="peer_dense",
    )(xt, u, vt, thr, c, s2, e2)


def _residual_ln_kernel(alpha, x_ref, pt_ref, g_ref, b_ref, o_ref, ob_ref):
    y = _layer_norm(alpha * x_ref[...] + pt_ref[...].T, g_ref[...], b_ref[...])
    o_ref[...] = y
    ob_ref[...] = y.astype(BF16)


def _residual_ln(alpha, x, pt, g, b, tm):
    n_tok, d = x.shape
    return pl.pallas_call(
        functools.partial(_residual_ln_kernel, alpha),
        grid=(n_tok // tm,),
        in_specs=[pl.BlockSpec((tm, d), lambda i: (i, 0)),
                  pl.BlockSpec((d, tm), lambda i: (0, i)),
                  pl.BlockSpec((1, d), lambda i: (0, 0)),
                  pl.BlockSpec((1, d), lambda i: (0, 0))],
        out_specs=[pl.BlockSpec((tm, d), lambda i: (i, 0))] * 2,
        out_shape=[jax.ShapeDtypeStruct((n_tok, d), F32), jax.ShapeDtypeStruct((n_tok, d), BF16)],
        compiler_params=_params(("parallel",), 40),
        name="residual_ln",
    )(x, pt, g, b)


def _block_diag(w, reps):
    return jnp.kron(jnp.eye(reps, dtype=w.dtype), w)


def kernel(x_prompt, x_sample, state_hgrn, w_in, sgu_ln_g, sgu_ln_b, sgu_w, sgu_b, hg_lb_logits,
           hg_norm_g, w_branch_a, w_branch_b, w_out, ln1_g, ln1_b, peer_w_q, peer_sub_keys,
           peer_u, peer_v, ln2_g, ln2_b):
    depth = w_in.shape[0]
    n_b, seq, d = x_prompt.shape
    n_s, dec_seq, _ = x_sample.shape
    width = sgu_ln_g.shape[1]
    n_prompt = n_b * seq
    alpha = (2 * depth) ** 0.25
    assert seq % ROWS == 0 and ROWS % dec_seq == 0 and n_s % (ROWS // dec_seq) == 0
    assert w_in.shape[2] == 6 * width + 2 * d and d == 2 * width

    x = jnp.concatenate([x_prompt.reshape(n_prompt, d), x_sample.reshape(n_s * dec_seq, d)], axis=0)
    xb = x.astype(BF16)
    row = lambda a: a.reshape(1, -1)

    tri = jnp.tril(jnp.ones((ROWS, ROWS), F32))
    tri_s = _block_diag(jnp.tril(jnp.ones((dec_seq, dec_seq), F32)), ROWS // dec_seq)

    new_sp, new_ss, new_v = [], [], []
    for l in range(depth):
        z = _matmul(xb, w_in[l].astype(BF16), 512, 1024, "in_proj")

        ws = jnp.where(tri > 0, sgu_w[l], 0.0)
        ws_s = jax.vmap(lambda w: _block_diag(w, ROWS // dec_seq))(ws[:, :dec_seq, :dec_seq])
        bias = jnp.repeat(sgu_b[l].T, width // SGU_GROUPS, axis=1)
        bias_s = jnp.tile(bias[:dec_seq], (ROWS // dec_seq, 1))
        common = (row(sgu_ln_g[l]), row(sgu_ln_b[l]))
        tail = (hg_lb_logits, row(hg_norm_g[l]))
        ya_p, yb_p, st_p = _mixer_prompt(l, z, (n_b, seq), width, *common, ws.astype(BF16), bias,
                                         tri.astype(BF16), *tail)
        ya_s, yb_s, vn_s, st_s = _mixer_sample(l, z, n_prompt, (n_s, dec_seq), width, *common,
                                               ws_s.astype(BF16), bias_s, tri_s.astype(BF16), *tail,
                                               state_hgrn[l])
        ya = jnp.concatenate([ya_p, ya_s], axis=0)
        yb = jnp.concatenate([yb_p, yb_s], axis=0)

        x1, x1t = _merge(alpha, ya, yb, z, x, w_branch_a[l].astype(BF16), w_branch_b[l].astype(BF16),
                         w_out[l].astype(BF16), row(ln1_g[l]), row(ln1_b[l]), 256,
                         (6 * width // d, 6 * width // d + 1))

        thr, c, s2, e2 = _peer_route(x1t, peer_w_q[l].T.astype(BF16), peer_sub_keys[l].astype(BF16), 256)
        pt = _peer_dense(x1t, peer_u[l].astype(BF16), peer_v[l].T.astype(BF16), thr, c, s2, e2, 512, 1024)
        x, xb = _residual_ln(alpha, x1, pt, row(ln2_g[l]), row(ln2_b[l]), 512)

        new_sp.append(st_p)
        new_ss.append(st_s)
        new_v.append(vn_s.reshape(n_s, dec_seq, width))

    return (x[:n_prompt].reshape(n_b, seq, d), x[n_prompt:].reshape(n_s, dec_seq, d),
            jnp.stack(new_sp), jnp.stack(new_ss), jnp.stack(new_v))
```

```python
import functools

import jax
import jax.numpy as jnp
from jax import lax
from jax.experimental import pallas as pl
from jax.experimental.pallas import tpu as pltpu

F32 = jnp.float32
BF16 = jnp.bfloat16

SGU_GROUPS = 4
HG_HEADS = 8
PEER_HEADS = 8
PEER_TOPK = 16
LN_EPS = 1e-5
RMS_EPS = 1e-6
ROWS = 128
HG_SUB = 16
MIB = 1024 * 1024

_NT = (((1,), (1,)), ((), ()))
_TN = (((0,), (0,)), ((), ()))


def _params(sem, vmem_mib):
    return pltpu.CompilerParams(dimension_semantics=sem, vmem_limit_bytes=vmem_mib * MIB)


def _gelu(x):
    c = 0.7978845608028654
    half = 0.5 * x
    return half + half * jnp.tanh(x * (c + (c * 0.044715) * (x * x)))


def _sigmoid(x):
    return 1.0 / (1.0 + jnp.exp(-x))


def _layer_norm(x, g, b):
    mu = jnp.mean(x, axis=-1, keepdims=True)
    xc = x - mu
    var = jnp.mean(xc * xc, axis=-1, keepdims=True)
    return xc * lax.rsqrt(var + LN_EPS) * g + b


def _bdot(a, b, dims=None):
    a = a.astype(BF16)
    b = b.astype(BF16)
    if dims is None:
        return jnp.dot(a, b, preferred_element_type=F32)
    return lax.dot_general(a, b, dims, preferred_element_type=F32)


def _matmul_kernel(x_ref, w_ref, o_ref):
    o_ref[...] = jnp.dot(x_ref[...], w_ref[...], preferred_element_type=F32)


def _matmul(x, w, tm, tn, name):
    m, k = x.shape
    n = w.shape[1]
    return pl.pallas_call(
        _matmul_kernel,
        grid=(n // tn, m // tm),
        in_specs=[pl.BlockSpec((tm, k), lambda j, i: (i, 0)),
                  pl.BlockSpec((k, tn), lambda j, i: (0, j))],
        out_specs=pl.BlockSpec((tm, tn), lambda j, i: (i, j)),
        out_shape=jax.ShapeDtypeStruct((m, n), F32),
        compiler_params=_params(("parallel", "parallel"), 40),
        name=name,
    )(x, w)


def _forget_lower_bound(logits_ref, layer):
    lg = logits_ref[...]
    e = jnp.exp(lg - jnp.max(lg, axis=0, keepdims=True))
    p = e / jnp.sum(e, axis=0, keepdims=True)
    return jnp.sum(p[1:layer + 1], axis=0, keepdims=True)


def _hgrn_gates(fp, lb):
    e = jnp.exp(-jnp.abs(fp))
    log_sig = jnp.minimum(fp, 0.0) - jnp.log1p(e)
    sig_neg = jnp.where(fp >= 0.0, e, 1.0) / (1.0 + e)
    if lb is None:
        return log_sig, sig_neg
    k = (1.0 - lb) * sig_neg
    pos = lb > 0.0
    a = jnp.log(jnp.where(pos, lb, 1.0))
    b = jnp.log1p(-lb) + log_sig
    lse = jnp.maximum(a, b) + jnp.log1p(jnp.exp(-jnp.abs(a - b)))
    return jnp.where(pos, lse, b), k


def _segment_cumsum(t_ref, x):
    t = t_ref[...]
    hi = x.astype(BF16)
    r1 = x - hi.astype(F32)
    mid = r1.astype(BF16)
    lo = (r1 - mid.astype(F32)).astype(BF16)
    dot = functools.partial(jnp.dot, preferred_element_type=F32)
    return dot(t, hi) + dot(t, mid) + dot(t, lo)


def _sgu(u, v, lng_ref, lnb_ref, ws_ref, bias_ref, ya_ref):
    vn = _layer_norm(_gelu(v), lng_ref[...], lnb_ref[...])
    vnb = vn.astype(BF16)
    gw = vn.shape[1] // SGU_GROUPS
    for g in range(SGU_GROUPS):
        cs = slice(g * gw, (g + 1) * gw)
        mixed = jnp.dot(ws_ref[g], vnb[:, cs], preferred_element_type=F32) + bias_ref[:, cs]
        ya_ref[:, cs] = (_gelu(u[:, cs]) * mixed).astype(BF16)
    return vn


def _hgrn_out(o, gn, og):
    o = o * lax.rsqrt(jnp.mean(o * o, axis=-1, keepdims=True) + RMS_EPS)
    return (o * gn * (og * _sigmoid(og))).astype(BF16)


def _rows_at(b_ref, size, offset, width):
    parts = []
    for j in range(ROWS // size):
        r = j * size + offset
        if r < 0:
            parts.append(jnp.zeros((size, width), F32))
        else:
            parts.append(jnp.broadcast_to(b_ref[pl.ds(r, 1), :], (size, width)))
    return parts[0] if len(parts) == 1 else jnp.concatenate(parts, axis=0)


def _mixer_prompt_kernel(layer, u_ref, v_ref, q_ref, f_ref, i_ref, og_ref, lng_ref, lnb_ref,
                         ws_ref, bias_ref, tri_ref, lbl_ref, gn_ref,
                         ya_ref, yb_ref, st_ref, stt_ref, b_ref):
    c = pl.program_id(1)
    width = q_ref.shape[1]
    dk = width // HG_HEADS

    @pl.when(c == 0)
    def _():
        stt_ref[...] = jnp.zeros_like(stt_ref)

    _sgu(u_ref[...], v_ref[...], lng_ref, lnb_ref, ws_ref, bias_ref, ya_ref)

    lb = None if layer == 0 else _forget_lower_bound(lbl_ref, layer)
    lf, k = _hgrn_gates(f_ref[...], lb)
    qs = q_ref[...] * (dk ** -0.5)
    b = _segment_cumsum(tri_ref, lf)
    b_ref[...] = b

    e_sub = b - _rows_at(b_ref, HG_SUB, -1, width)
    q_lvl = [qs * jnp.exp(e_sub)]
    k_lvl = [k * jnp.exp(-e_sub)]
    size = HG_SUB
    sizes = []
    while size < ROWS:
        sizes.append(size)
        q_lvl.append(q_lvl[0] if size == HG_SUB else qs * jnp.exp(b - _rows_at(b_ref, size, -1, width)))
        k_lvl.append(k * jnp.exp(_rows_at(b_ref, size, size - 1, width) - b))
        size *= 2
    qd = qs * jnp.exp(b)
    b_end = b_ref[pl.ds(ROWS - 1, 1), :]
    ke = k * jnp.exp(b_end - b)
    g_end = jnp.exp(b_end)

    row = lax.broadcasted_iota(jnp.int32, (ROWS, ROWS), 0)
    col = lax.broadcasted_iota(jnp.int32, (ROWS, ROWS), 1)
    masks = [(row // HG_SUB == col // HG_SUB) & (col <= row)]
    for sz in sizes:
        masks.append(((row // sz) % 2 == 1) & (col // sz == row // sz - 1))

    vv = i_ref[...]
    gn = gn_ref[...]
    og = og_ref[...]
    for h in range(HG_HEADS):
        hs = slice(h * dk, (h + 1) * dk)
        attn = jnp.zeros((ROWS, ROWS), F32)
        for ql, kl, mk in zip(q_lvl, k_lvl, masks):
            attn = jnp.where(mk, _bdot(ql[:, hs], kl[:, hs], _NT), attn)
        vh = vv[:, hs]
        stt = stt_ref[h]
        o = _bdot(attn, vh) + _bdot(qd[:, hs], stt, _NT)
        stt_ref[h] = stt * g_end[:, hs] + _bdot(vh, ke[:, hs], _TN)
        yb_ref[:, hs] = _hgrn_out(o, gn[:, hs], og[:, hs])

    @pl.when(c == pl.num_programs(1) - 1)
    def _():
        for h in range(HG_HEADS):
            st_ref[0, h] = stt_ref[h].T


def _mixer_sample_kernel(layer, seq_len, u_ref, v_ref, q_ref, f_ref, i_ref, og_ref, lng_ref, lnb_ref,
                         ws_ref, bias_ref, tri_ref, lbl_ref, gn_ref, s0_ref,
                         ya_ref, yb_ref, vn_ref, st_ref, b_ref, oi_ref):
    width = q_ref.shape[1]
    dk = width // HG_HEADS
    n_seq = ROWS // seq_len

    vn_ref[...] = _sgu(u_ref[...], v_ref[...], lng_ref, lnb_ref, ws_ref, bias_ref, ya_ref)

    lb = None if layer == 0 else _forget_lower_bound(lbl_ref, layer)
    lf, k = _hgrn_gates(f_ref[...], lb)
    qs = q_ref[...] * (dk ** -0.5)
    b = _segment_cumsum(tri_ref, lf)
    b_ref[...] = b
    qd = qs * jnp.exp(b)
    k0 = k * jnp.exp(-b)
    ke = k * jnp.exp(_rows_at(b_ref, seq_len, seq_len - 1, width) - b)

    row = lax.broadcasted_iota(jnp.int32, (ROWS, ROWS), 0)
    col = lax.broadcasted_iota(jnp.int32, (ROWS, ROWS), 1)
    causal = (row // seq_len == col // seq_len) & (col <= row)
    seq_of_row = lax.broadcasted_iota(jnp.int32, (ROWS, dk), 0) // seq_len

    vv = i_ref[...]
    gn = gn_ref[...]
    og = og_ref[...]
    for h in range(HG_HEADS):
        hs = slice(h * dk, (h + 1) * dk)
        attn = jnp.where(causal, _bdot(qd[:, hs], k0[:, hs], _NT), 0.0)
        vh = vv[:, hs]
        qdh = qd[:, hs].astype(BF16)
        keh = ke[:, hs].astype(BF16)
        oi_ref[...] = _bdot(attn, vh)

        def per_seq(s, carry, h=h, hs=hs, vh=vh, qdh=qdh, keh=keh):
            mine = seq_of_row == s
            s0 = s0_ref[s, h]
            oi_ref[...] += jnp.where(mine, _bdot(qdh, s0), 0.0)
            b_seq = b_ref[pl.ds(pl.multiple_of(s * seq_len, seq_len), seq_len), hs]
            g_end = jnp.exp(b_seq[seq_len - 1:seq_len])
            new_t = s0.T * g_end + _bdot(jnp.where(mine, vh, 0.0), keh, _TN)
            st_ref[s, h] = new_t.T
            return carry

        lax.fori_loop(0, n_seq, per_seq, 0, unroll=4)
        yb_ref[:, hs] = _hgrn_out(oi_ref[...], gn[:, hs], og[:, hs])


def _mixer_specs(width, row_block, n_axes):
    def idx(f):
        return f if n_axes == 2 else (lambda t: f(t, 0))

    def zcol(j):
        return pl.BlockSpec((ROWS, width), idx(lambda a, c, j=j: (row_block(a, c), j)))

    def const(shape):
        return pl.BlockSpec(shape, idx(lambda a, c: (0,) * len(shape)))

    return zcol, const


def _mixer_prompt(layer, z, batch, width, lng, lnb, ws, bias, tri, lbl, gn):
    n_chunks = batch[1] // ROWS
    zcol, const = _mixer_specs(width, lambda a, c: a * n_chunks + c, 2)
    depth = lbl.shape[0]
    dk = width // HG_HEADS
    n_tok = batch[0] * batch[1]
    return pl.pallas_call(
        functools.partial(_mixer_prompt_kernel, layer),
        grid=(batch[0], n_chunks),
        in_specs=[zcol(j) for j in range(6)] + [
            const((1, width)), const((1, width)), const((SGU_GROUPS, ROWS, ROWS)),
            const((ROWS, width)), const((ROWS, ROWS)), const((depth, width)), const((1, width))],
        out_specs=[pl.BlockSpec((ROWS, width), lambda a, c: (a * n_chunks + c, 0)),
                   pl.BlockSpec((ROWS, width), lambda a, c: (a * n_chunks + c, 0)),
                   pl.BlockSpec((1, HG_HEADS, dk, dk), lambda a, c: (a, 0, 0, 0))],
        out_shape=[jax.ShapeDtypeStruct((n_tok, width), BF16),
                   jax.ShapeDtypeStruct((n_tok, width), BF16),
                   jax.ShapeDtypeStruct((batch[0], HG_HEADS, dk, dk), F32)],
        scratch_shapes=[pltpu.VMEM((HG_HEADS, dk, dk), F32), pltpu.VMEM((ROWS, width), F32)],
        compiler_params=_params(("parallel", "arbitrary"), 56),
        name="mixer_prompt",
    )(z, z, z, z, z, z, lng, lnb, ws, bias, tri, lbl, gn)


def _mixer_sample(layer, z, row0, batch, width, lng, lnb, ws, bias, tri, lbl, gn, s0):
    n_seq, seq_len = batch
    per_step = ROWS // seq_len
    steps = n_seq // per_step
    blk0 = row0 // ROWS
    zcol, const = _mixer_specs(width, lambda t, c: blk0 + t, 1)
    depth = lbl.shape[0]
    dk = width // HG_HEADS
    n_tok = n_seq * seq_len
    st_spec = pl.BlockSpec((per_step, HG_HEADS, dk, dk), lambda t: (t, 0, 0, 0))
    row_spec = pl.BlockSpec((ROWS, width), lambda t: (t, 0))
    return pl.pallas_call(
        functools.partial(_mixer_sample_kernel, layer, seq_len),
        grid=(steps,),
        in_specs=[zcol(j) for j in range(6)] + [
            const((1, width)), const((1, width)), const((SGU_GROUPS, ROWS, ROWS)),
            const((ROWS, width)), const((ROWS, ROWS)), const((depth, width)), const((1, width)),
            st_spec],
        out_specs=[row_spec, row_spec, row_spec, st_spec],
        out_shape=[jax.ShapeDtypeStruct((n_tok, width), BF16),
                   jax.ShapeDtypeStruct((n_tok, width), BF16),
                   jax.ShapeDtypeStruct((n_tok, width), F32),
                   jax.ShapeDtypeStruct((n_seq, HG_HEADS, dk, dk), F32)],
        scratch_shapes=[pltpu.VMEM((ROWS, width), F32), pltpu.VMEM((ROWS, dk), F32)],
        compiler_params=_params(("parallel",), 58),
        name="mixer_sample",
    )(z, z, z, z, z, z, lng, lnb, ws, bias, tri, lbl, gn, s0)


def _merge_kernel(alpha, ya_ref, yb_ref, ga_ref, gb_ref, x_ref, wpa_ref, wpb_ref, wo_ref, g_ref, b_ref,
                  x1_ref, x1t_ref):
    a = jnp.dot(ya_ref[...], wpa_ref[...], preferred_element_type=F32)
    b = jnp.dot(yb_ref[...], wpb_ref[...], preferred_element_type=F32)
    merged = _sigmoid(ga_ref[...]) * a + _sigmoid(gb_ref[...]) * b
    y = jnp.dot(merged.astype(BF16), wo_ref[...], preferred_element_type=F32)
    x1 = _layer_norm(alpha * x_ref[...] + y, g_ref[...], b_ref[...])
    x1_ref[...] = x1
    x1t_ref[...] = x1.T.astype(BF16)


def _merge(alpha, ya, yb, z, x, wpa, wpb, wo, g, b, tm, gate_blocks):
    n_tok, d = x.shape
    da = ya.shape[1]
    ga_blk, gb_blk = gate_blocks
    const = lambda shape: pl.BlockSpec(shape, lambda i: (0,) * len(shape), pipeline_mode=pl.Buffered(1))
    return pl.pallas_call(
        functools.partial(_merge_kernel, alpha),
        grid=(n_tok // tm,),
        in_specs=[pl.BlockSpec((tm, da), lambda i: (i, 0)),
                  pl.BlockSpec((tm, da), lambda i: (i, 0)),
                  pl.BlockSpec((tm, d), lambda i: (i, ga_blk)),
                  pl.BlockSpec((tm, d), lambda i: (i, gb_blk)),
                  pl.BlockSpec((tm, d), lambda i: (i, 0)),
                  const((da, d)), const((da, d)), const((d, d)), const((1, d)), const((1, d))],
        out_specs=[pl.BlockSpec((tm, d), lambda i: (i, 0)),
                   pl.BlockSpec((d, tm), lambda i: (0, i))],
        out_shape=[jax.ShapeDtypeStruct((n_tok, d), F32),
                   jax.ShapeDtypeStruct((d, n_tok), BF16)],
        compiler_params=_params(("parallel",), 58),
        name="merge_ln",
    )(ya, yb, z, z, x, wpa, wpb, wo, g, b)


def _top_values(a, count):
    out = []
    for _ in range(count):
        m = jnp.max(a, axis=0, keepdims=True)
        out.append(m)
        a = jnp.where(a == m, -jnp.inf, a)
    return out


def _stack_rows(rows, n):
    lanes = rows[0].shape[1]
    idx = lax.broadcasted_iota(jnp.int32, (n, lanes), 0)
    acc = jnp.broadcast_to(rows[0], (n, lanes))
    for r in range(1, n):
        acc = jnp.where(idx == r, rows[r], acc)
    return acc


def _peer_route_kernel(xt_ref, wqt_ref, sk_ref, thr_ref, c_ref, s2_ref, e2_ref, q_ref):
    q_ref[...] = jnp.dot(wqt_ref[...], xt_ref[...], preferred_element_type=F32).astype(BF16)
    half = sk_ref.shape[3]

    def per_head(h, carry):
        base = pl.multiple_of(h * 2 * half, 2 * half)
        s1 = jnp.dot(sk_ref[h, 0], q_ref[pl.ds(base, half), :], preferred_element_type=F32)
        s2 = jnp.dot(sk_ref[h, 1], q_ref[pl.ds(base + half, half), :], preferred_element_type=F32)
        t1 = _top_values(s1, PEER_TOPK)
        t2 = _top_values(s2, PEER_TOPK)
        t2_all = _stack_rows(t2, PEER_TOPK)
        t2_half = t2_all[:PEER_TOPK // 2]
        sums = [t1[a] + t2[b] for a in range(3, PEER_TOPK) for b in range(PEER_TOPK // (a + 1))]
        sums += [jnp.full_like(t1[0], -jnp.inf)] * (-len(sums) % 8)
        packed = [_stack_rows(sums[r:r + 8], 8) for r in range(0, len(sums), 8)]
        cand = jnp.concatenate([t1[0] + t2_all, t1[1] + t2_half, t1[2] + t2_half] + packed, axis=0)
        above = jnp.zeros_like(t1[0])
        tau = t1[0] + t2[0]
        for _ in range(PEER_TOPK):
            m = jnp.max(cand, axis=0, keepdims=True)
            eq = cand == m
            tau = jnp.where(above < PEER_TOPK, m, tau)
            above = above + jnp.sum(jnp.where(eq, 1.0, 0.0), axis=0, keepdims=True)
            cand = jnp.where(eq, -jnp.inf, cand)
        rows = [t1[a] + t2_all for a in range(PEER_TOPK)]
        count = lambda mask: jnp.sum(jnp.where(mask, 1.0, 0.0), axis=0, keepdims=True)
        n_gt = [count(r > tau) for r in rows]
        n_eq = [count(r == tau) for r in rows]
        spare = float(PEER_TOPK) - sum(n_gt)
        rank = lax.broadcasted_iota(jnp.int32, t2_all.shape, 0).astype(F32)
        top_score = t1[0] + t2[0]
        z = jnp.zeros_like(tau)
        thr = jnp.full(s1.shape, jnp.inf, F32)
        for a in range(PEER_TOPK):
            take = jnp.minimum(n_eq[a], spare)
            spare = spare - take
            keep = rank < n_gt[a] + take
            z = z + jnp.sum(jnp.where(keep, jnp.exp(rows[a] - top_score), 0.0), axis=0, keepdims=True)
            lowest_kept = jnp.min(jnp.where(keep, t2_all, jnp.inf), axis=0, keepdims=True)
            thr = jnp.where(s1 == t1[a], lowest_kept, thr)
        thr_ref[h] = thr
        c_ref[h] = jnp.exp(s1 - t1[0]) / z
        s2_ref[h] = s2
        e2_ref[h] = jnp.exp(s2 - t2[0])
        return carry

    lax.fori_loop(0, PEER_HEADS, per_head, 0)


def _peer_route(xt, wqt, sk, tb):
    d, n_tok = xt.shape
    nk = sk.shape[2]
    out = jax.ShapeDtypeStruct((PEER_HEADS, nk, n_tok), F32)
    ospec = pl.BlockSpec((PEER_HEADS, nk, tb), lambda i: (0, 0, i))
    return pl.pallas_call(
        _peer_route_kernel,
        grid=(n_tok // tb,),
        in_specs=[pl.BlockSpec((d, tb), lambda i: (0, i)),
                  pl.BlockSpec(wqt.shape, lambda i: (0, 0)),
                  pl.BlockSpec(sk.shape, lambda i: (0, 0, 0, 0))],
        out_specs=[ospec] * 4,
        out_shape=[out] * 4,
        scratch_shapes=[pltpu.VMEM((wqt.shape[0], tb), BF16)],
        compiler_params=_params(("parallel",), 48),
        name="peer_route",
    )(xt, wqt, sk)


def _peer_dense_kernel(n_tiles, n_exp_tiles, lane_chunk, xt_ref, u_ref, vt_ref, thr_ref, c_ref,
                       s2_ref, e2_ref, out_ref, ht_a, ht_b, pt_a, pt_b):
    s = pl.program_id(0)
    retrieved = jnp.clip(s - 2, 0, n_tiles - 1)

    @pl.when(s == 0)
    def _():
        for ref in (ht_a, ht_b, pt_a, pt_b):
            ref[...] = jnp.zeros_like(ref)

    @pl.when(retrieved % n_exp_tiles == 0)
    def _():
        out_ref[...] = jnp.zeros_like(out_ref)

    nk = s2_ref.shape[1]
    tb = xt_ref.shape[1]
    n_keys = u_ref.shape[0] // nk
    sub = 16

    def stages(ht_new, ht_old, pt_new, pt_old):
        def project(rs, ls):
            ht_new[rs, ls] = jnp.dot(u_ref[rs, :], xt_ref[:, ls], preferred_element_type=F32)

        def gate(keys, j0, ls):
            acc = {}
            for h in range(PEER_HEADS):
                s2 = s2_ref[h, j0:j0 + sub, ls]
                e2 = e2_ref[h, j0:j0 + sub, ls]
                for ii in keys:
                    hit = s2 >= thr_ref[h, ii:ii + 1, ls]
                    term = jnp.where(hit, e2, 0.0) * c_ref[h, ii:ii + 1, ls]
                    acc[ii] = term if h == 0 else acc[ii] + term
            for ii in keys:
                rs = slice(ii * nk + j0, ii * nk + j0 + sub)
                pt_new[rs, ls] = (acc[ii] * _gelu(ht_old[rs, ls])).astype(BF16)

        def retrieve(rs, ls):
            out_ref[rs, ls] += jnp.dot(vt_ref[rs, :], pt_old[:, ls], preferred_element_type=F32)

        lanes = [slice(t0, t0 + lane_chunk) for t0 in range(0, tb, lane_chunk)]
        blk = 256
        p_units = [functools.partial(project, slice(r0, r0 + blk), ls)
                   for ls in lanes for r0 in range(0, u_ref.shape[0], blk)]
        r_units = [functools.partial(retrieve, slice(r0, r0 + blk), ls)
                   for ls in lanes for r0 in range(0, vt_ref.shape[0], blk)]
        key_groups = [range(n_keys)]
        g_units = [functools.partial(gate, keys, j0, slice(g0, g0 + 128))
                   for g0 in range(0, tb, 128) for j0 in range(0, nk, sub) for keys in key_groups]
        for n, g_unit in enumerate(g_units):
            for units in (p_units, r_units):
                if n * len(units) % len(g_units) == 0:
                    units[n * len(units) // len(g_units)]()
            g_unit()

    @pl.when(s % 2 == 0)
    def _():
        stages(ht_a, ht_b, pt_b, pt_a)

    @pl.when(s % 2 == 1)
    def _():
        stages(ht_b, ht_a, pt_a, pt_b)


def _table_prep_kernel(u_ref, v_ref, ub_ref, vt_ref):
    ub_ref[...] = u_ref[...].astype(BF16)
    vt_ref[...] = v_ref[...].T.astype(BF16)


def _table_prep(peer_u, peer_v, eb):
    depth, n_exp, d = peer_u.shape
    in_spec = pl.BlockSpec((None, eb, d), lambda l, e: (l, e, 0))
    return pl.pallas_call(
        _table_prep_kernel,
        grid=(depth, n_exp // eb),
        in_specs=[in_spec, in_spec],
        out_specs=[pl.BlockSpec((None, eb, d), lambda l, e: (l, e, 0)),
                   pl.BlockSpec((None, d, eb), lambda l, e: (l, 0, e))],
        out_shape=[jax.ShapeDtypeStruct((depth, n_exp, d), BF16),
                   jax.ShapeDtypeStruct((depth, d, n_exp), BF16)],
        compiler_params=_params(("parallel", "parallel"), 48),
        name="table_prep",
    )(peer_u, peer_v)


def _peer_dense(layer, xt, u, vt, thr, c, s2, e2, tb, eb):
    d, n_tok = xt.shape
    n_exp = u.shape[1]
    nk = s2.shape[1]
    keys_per_blk = eb // nk
    n_e = n_exp // eb
    n_tiles = (n_tok // tb) * n_e
    proj = lambda s: jnp.minimum(s, n_tiles - 1)
    gate = lambda s: jnp.clip(s - 1, 0, n_tiles - 1)
    retr = lambda s: jnp.clip(s - 2, 0, n_tiles - 1)
    row_spec = pl.BlockSpec((PEER_HEADS, keys_per_blk, tb), lambda s: (0, gate(s) % n_e, gate(s) // n_e))
    all_spec = pl.BlockSpec((PEER_HEADS, nk, tb), lambda s: (0, 0, gate(s) // n_e))
    return pl.pallas_call(
        functools.partial(_peer_dense_kernel, n_tiles, n_e, 256),
        grid=(n_tiles + 2,),
        in_specs=[pl.BlockSpec((d, tb), lambda s: (0, proj(s) // n_e)),
                  pl.BlockSpec((None, eb, d), lambda s: (layer, proj(s) % n_e, 0)),
                  pl.BlockSpec((None, d, eb), lambda s: (layer, 0, retr(s) % n_e)),
                  row_spec, row_spec, all_spec, all_spec],
        out_specs=pl.BlockSpec((d, tb), lambda s: (0, retr(s) // n_e)),
        out_shape=jax.ShapeDtypeStruct((d, n_tok), F32),
        scratch_shapes=[pltpu.VMEM((eb, tb), F32), pltpu.VMEM((eb, tb), F32),
                        pltpu.VMEM((eb, tb), BF16), pltpu.VMEM((eb, tb), BF16)],
        compiler_params=_params(("arbitrary",), 58),
        name="peer_dense",
    )(xt, u, vt, thr, c, s2, e2)


def _residual_ln_kernel(alpha, x_ref, pt_ref, g_ref, b_ref, o_ref, ob_ref):
    y = _layer_norm(alpha * x_ref[...] + pt_ref[...].T, g_ref[...], b_ref[...])
    o_ref[...] = y
    ob_ref[...] = y.astype(BF16)


def _residual_ln(alpha, x, pt, g, b, tm):
    n_tok, d = x.shape
    return pl.pallas_call(
        functools.partial(_residual_ln_kernel, alpha),
        grid=(n_tok // tm,),
        in_specs=[pl.BlockSpec((tm, d), lambda i: (i, 0)),
                  pl.BlockSpec((d, tm), lambda i: (0, i)),
                  pl.BlockSpec((1, d), lambda i: (0, 0)),
                  pl.BlockSpec((1, d), lambda i: (0, 0))],
        out_specs=[pl.BlockSpec((tm, d), lambda i: (i, 0))] * 2,
        out_shape=[jax.ShapeDtypeStruct((n_tok, d), F32), jax.ShapeDtypeStruct((n_tok, d), BF16)],
        compiler_params=_params(("parallel",), 40),
        name="residual_ln",
    )(x, pt, g, b)


def _block_diag(w, reps):
    return jnp.kron(jnp.eye(reps, dtype=w.dtype), w)


def kernel(x_prompt, x_sample, state_hgrn, w_in, sgu_ln_g, sgu_ln_b, sgu_w, sgu_b, hg_lb_logits,
           hg_norm_g, w_branch_a, w_branch_b, w_out, ln1_g, ln1_b, peer_w_q, peer_sub_keys,
           peer_u, peer_v, ln2_g, ln2_b):
    depth = w_in.shape[0]
    n_b, seq, d = x_prompt.shape
    n_s, dec_seq, _ = x_sample.shape
    width = sgu_ln_g.shape[1]
    n_prompt = n_b * seq
    alpha = (2 * depth) ** 0.25
    assert seq % ROWS == 0 and ROWS % dec_seq == 0 and n_s % (ROWS // dec_seq) == 0
    assert w_in.shape[2] == 6 * width + 2 * d and d == 2 * width

    x = jnp.concatenate([x_prompt.reshape(n_prompt, d), x_sample.reshape(n_s * dec_seq, d)], axis=0)
    xb = x.astype(BF16)
    row = lambda a: a.reshape(1, -1)

    tri = jnp.tril(jnp.ones((ROWS, ROWS), F32))
    tri_s = _block_diag(jnp.tril(jnp.ones((dec_seq, dec_seq), F32)), ROWS // dec_seq)

    u_tab, vt_tab = _table_prep(peer_u, peer_v, 512)

    new_sp, new_ss, new_v = [], [], []
    for l in range(depth):
        z = _matmul(xb, w_in[l].astype(BF16), 512, 1024, "in_proj")

        ws = jnp.where(tri > 0, sgu_w[l], 0.0)
        ws_s = jax.vmap(lambda w: _block_diag(w, ROWS // dec_seq))(ws[:, :dec_seq, :dec_seq])
        bias = jnp.repeat(sgu_b[l].T, width // SGU_GROUPS, axis=1)
        bias_s = jnp.tile(bias[:dec_seq], (ROWS // dec_seq, 1))
        common = (row(sgu_ln_g[l]), row(sgu_ln_b[l]))
        tail = (hg_lb_logits, row(hg_norm_g[l]))
        ya_p, yb_p, st_p = _mixer_prompt(l, z, (n_b, seq), width, *common, ws.astype(BF16), bias,
                                         tri.astype(BF16), *tail)
        ya_s, yb_s, vn_s, st_s = _mixer_sample(l, z, n_prompt, (n_s, dec_seq), width, *common,
                                               ws_s.astype(BF16), bias_s, tri_s.astype(BF16), *tail,
                                               state_hgrn[l])
        ya = jnp.concatenate([ya_p, ya_s], axis=0)
        yb = jnp.concatenate([yb_p, yb_s], axis=0)

        x1, x1t = _merge(alpha, ya, yb, z, x, w_branch_a[l].astype(BF16), w_branch_b[l].astype(BF16),
                         w_out[l].astype(BF16), row(ln1_g[l]), row(ln1_b[l]), 256,
                         (6 * width // d, 6 * width // d + 1))

        thr, c, s2, e2 = _peer_route(x1t, peer_w_q[l].T.astype(BF16), peer_sub_keys[l].astype(BF16), 256)
        pt = _peer_dense(l, x1t, u_tab, vt_tab, thr, c, s2, e2, 512, 1024)
        x, xb = _residual_ln(alpha, x1, pt, row(ln2_g[l]), row(ln2_b[l]), 512)

        new_sp.append(st_p)
        new_ss.append(st_s)
        new_v.append(vn_s.reshape(n_s, dec_seq, width))

    return (x[:n_prompt].reshape(n_b, seq, d), x[n_prompt:].reshape(n_s, dec_seq, d),
            jnp.stack(new_sp), jnp.stack(new_ss), jnp.stack(new_v))
```

```python
import functools

import jax
import jax.numpy as jnp
from jax import lax
from jax.experimental import pallas as pl
from jax.experimental.pallas import tpu as pltpu

F32 = jnp.float32
BF16 = jnp.bfloat16

SGU_GROUPS = 4
HG_HEADS = 8
PEER_HEADS = 8
PEER_TOPK = 16
LN_EPS = 1e-5
RMS_EPS = 1e-6
ROWS = 128
HG_SUB = 16
MIB = 1024 * 1024

_NT = (((1,), (1,)), ((), ()))
_TN = (((0,), (0,)), ((), ()))


def _params(sem, vmem_mib):
    return pltpu.CompilerParams(dimension_semantics=sem, vmem_limit_bytes=vmem_mib * MIB)


def _gelu(x):
    c = 0.7978845608028654
    half = 0.5 * x
    return half + half * jnp.tanh(x * (c + (c * 0.044715) * (x * x)))


def _sigmoid(x):
    return 1.0 / (1.0 + jnp.exp(-x))


def _layer_norm(x, g, b):
    mu = jnp.mean(x, axis=-1, keepdims=True)
    xc = x - mu
    var = jnp.mean(xc * xc, axis=-1, keepdims=True)
    return xc * lax.rsqrt(var + LN_EPS) * g + b


def _bdot(a, b, dims=None):
    a = a.astype(BF16)
    b = b.astype(BF16)
    if dims is None:
        return jnp.dot(a, b, preferred_element_type=F32)
    return lax.dot_general(a, b, dims, preferred_element_type=F32)


def _matmul_kernel(x_ref, w_ref, o_ref):
    o_ref[...] = jnp.dot(x_ref[...], w_ref[...], preferred_element_type=F32)


def _matmul(x, w, tm, tn, name):
    m, k = x.shape
    n = w.shape[1]
    return pl.pallas_call(
        _matmul_kernel,
        grid=(n // tn, m // tm),
        in_specs=[pl.BlockSpec((tm, k), lambda j, i: (i, 0)),
                  pl.BlockSpec((k, tn), lambda j, i: (0, j))],
        out_specs=pl.BlockSpec((tm, tn), lambda j, i: (i, j)),
        out_shape=jax.ShapeDtypeStruct((m, n), F32),
        compiler_params=_params(("parallel", "parallel"), 40),
        name=name,
    )(x, w)


def _forget_lower_bound(logits_ref, layer):
    lg = logits_ref[...]
    e = jnp.exp(lg - jnp.max(lg, axis=0, keepdims=True))
    p = e / jnp.sum(e, axis=0, keepdims=True)
    return jnp.sum(p[1:layer + 1], axis=0, keepdims=True)


def _hgrn_gates(fp, lb):
    e = jnp.exp(-jnp.abs(fp))
    log_sig = jnp.minimum(fp, 0.0) - jnp.log1p(e)
    sig_neg = jnp.where(fp >= 0.0, e, 1.0) / (1.0 + e)
    if lb is None:
        return log_sig, sig_neg
    k = (1.0 - lb) * sig_neg
    pos = lb > 0.0
    a = jnp.log(jnp.where(pos, lb, 1.0))
    b = jnp.log1p(-lb) + log_sig
    lse = jnp.maximum(a, b) + jnp.log1p(jnp.exp(-jnp.abs(a - b)))
    return jnp.where(pos, lse, b), k


def _segment_cumsum(t_ref, x):
    t = t_ref[...]
    hi = x.astype(BF16)
    r1 = x - hi.astype(F32)
    mid = r1.astype(BF16)
    lo = (r1 - mid.astype(F32)).astype(BF16)
    dot = functools.partial(jnp.dot, preferred_element_type=F32)
    return dot(t, hi) + dot(t, mid) + dot(t, lo)


def _sgu(u, v, lng_ref, lnb_ref, ws_ref, bias_ref, ya_ref):
    vn = _layer_norm(_gelu(v), lng_ref[...], lnb_ref[...])
    vnb = vn.astype(BF16)
    gw = vn.shape[1] // SGU_GROUPS
    for g in range(SGU_GROUPS):
        cs = slice(g * gw, (g + 1) * gw)
        mixed = jnp.dot(ws_ref[g], vnb[:, cs], preferred_element_type=F32) + bias_ref[:, cs]
        ya_ref[:, cs] = (_gelu(u[:, cs]) * mixed).astype(BF16)
    return vn


def _hgrn_out(o, gn, og):
    o = o * lax.rsqrt(jnp.mean(o * o, axis=-1, keepdims=True) + RMS_EPS)
    return (o * gn * (og * _sigmoid(og))).astype(BF16)


def _rows_at(b_ref, size, offset, width):
    parts = []
    for j in range(ROWS // size):
        r = j * size + offset
        if r < 0:
            parts.append(jnp.zeros((size, width), F32))
        else:
            parts.append(jnp.broadcast_to(b_ref[pl.ds(r, 1), :], (size, width)))
    return parts[0] if len(parts) == 1 else jnp.concatenate(parts, axis=0)


def _mixer_prompt_kernel(layer, u_ref, v_ref, q_ref, f_ref, i_ref, og_ref, lng_ref, lnb_ref,
                         ws_ref, bias_ref, tri_ref, lbl_ref, gn_ref,
                         ya_ref, yb_ref, st_ref, stt_ref, b_ref):
    c = pl.program_id(1)
    width = q_ref.shape[1]
    dk = width // HG_HEADS

    @pl.when(c == 0)
    def _():
        stt_ref[...] = jnp.zeros_like(stt_ref)

    _sgu(u_ref[...], v_ref[...], lng_ref, lnb_ref, ws_ref, bias_ref, ya_ref)

    lb = None if layer == 0 else _forget_lower_bound(lbl_ref, layer)
    lf, k = _hgrn_gates(f_ref[...], lb)
    qs = q_ref[...] * (dk ** -0.5)
    b = _segment_cumsum(tri_ref, lf)
    b_ref[...] = b

    e_sub = b - _rows_at(b_ref, HG_SUB, -1, width)
    q_lvl = [qs * jnp.exp(e_sub)]
    k_lvl = [k * jnp.exp(-e_sub)]
    size = HG_SUB
    sizes = []
    while size < ROWS:
        sizes.append(size)
        q_lvl.append(q_lvl[0] if size == HG_SUB else qs * jnp.exp(b - _rows_at(b_ref, size, -1, width)))
        k_lvl.append(k * jnp.exp(_rows_at(b_ref, size, size - 1, width) - b))
        size *= 2
    qd = qs * jnp.exp(b)
    b_end = b_ref[pl.ds(ROWS - 1, 1), :]
    ke = k * jnp.exp(b_end - b)
    g_end = jnp.exp(b_end)

    row = lax.broadcasted_iota(jnp.int32, (ROWS, ROWS), 0)
    col = lax.broadcasted_iota(jnp.int32, (ROWS, ROWS), 1)
    masks = [(row // HG_SUB == col // HG_SUB) & (col <= row)]
    for sz in sizes:
        masks.append(((row // sz) % 2 == 1) & (col // sz == row // sz - 1))

    vv = i_ref[...]
    gn = gn_ref[...]
    og = og_ref[...]
    for h in range(HG_HEADS):
        hs = slice(h * dk, (h + 1) * dk)
        attn = jnp.zeros((ROWS, ROWS), F32)
        for ql, kl, mk in zip(q_lvl, k_lvl, masks):
            attn = jnp.where(mk, _bdot(ql[:, hs], kl[:, hs], _NT), attn)
        vh = vv[:, hs]
        stt = stt_ref[h]
        o = _bdot(attn, vh) + _bdot(qd[:, hs], stt, _NT)
        stt_ref[h] = stt * g_end[:, hs] + _bdot(vh, ke[:, hs], _TN)
        yb_ref[:, hs] = _hgrn_out(o, gn[:, hs], og[:, hs])

    @pl.when(c == pl.num_programs(1) - 1)
    def _():
        for h in range(HG_HEADS):
            st_ref[0, h] = stt_ref[h].T


def _mixer_sample_kernel(layer, seq_len, u_ref, v_ref, q_ref, f_ref, i_ref, og_ref, lng_ref, lnb_ref,
                         ws_ref, bias_ref, tri_ref, lbl_ref, gn_ref, s0_ref,
                         ya_ref, yb_ref, vn_ref, st_ref, b_ref, oi_ref):
    width = q_ref.shape[1]
    dk = width // HG_HEADS
    n_seq = ROWS // seq_len

    vn_ref[...] = _sgu(u_ref[...], v_ref[...], lng_ref, lnb_ref, ws_ref, bias_ref, ya_ref)

    lb = None if layer == 0 else _forget_lower_bound(lbl_ref, layer)
    lf, k = _hgrn_gates(f_ref[...], lb)
    qs = q_ref[...] * (dk ** -0.5)
    b = _segment_cumsum(tri_ref, lf)
    b_ref[...] = b
    qd = qs * jnp.exp(b)
    k0 = k * jnp.exp(-b)
    ke = k * jnp.exp(_rows_at(b_ref, seq_len, seq_len - 1, width) - b)

    row = lax.broadcasted_iota(jnp.int32, (ROWS, ROWS), 0)
    col = lax.broadcasted_iota(jnp.int32, (ROWS, ROWS), 1)
    causal = (row // seq_len == col // seq_len) & (col <= row)
    seq_of_row = lax.broadcasted_iota(jnp.int32, (ROWS, dk), 0) // seq_len

    vv = i_ref[...]
    gn = gn_ref[...]
    og = og_ref[...]
    for h in range(HG_HEADS):
        hs = slice(h * dk, (h + 1) * dk)
        attn = jnp.where(causal, _bdot(qd[:, hs], k0[:, hs], _NT), 0.0)
        vh = vv[:, hs]
        qdh = qd[:, hs].astype(BF16)
        keh = ke[:, hs].astype(BF16)
        oi_ref[...] = _bdot(attn, vh)

        def per_seq(s, carry, h=h, hs=hs, vh=vh, qdh=qdh, keh=keh):
            mine = seq_of_row == s
            s0 = s0_ref[s, h]
            oi_ref[...] += jnp.where(mine, _bdot(qdh, s0), 0.0)
            b_seq = b_ref[pl.ds(pl.multiple_of(s * seq_len, seq_len), seq_len), hs]
            g_end = jnp.exp(b_seq[seq_len - 1:seq_len])
            new_t = s0.T * g_end + _bdot(jnp.where(mine, vh, 0.0), keh, _TN)
            st_ref[s, h] = new_t.T
            return carry

        lax.fori_loop(0, n_seq, per_seq, 0, unroll=4)
        yb_ref[:, hs] = _hgrn_out(oi_ref[...], gn[:, hs], og[:, hs])


def _mixer_specs(width, row_block, n_axes):
    def idx(f):
        return f if n_axes == 2 else (lambda t: f(t, 0))

    def zcol(j):
        return pl.BlockSpec((ROWS, width), idx(lambda a, c, j=j: (row_block(a, c), j)))

    def const(shape):
        return pl.BlockSpec(shape, idx(lambda a, c: (0,) * len(shape)))

    return zcol, const


def _mixer_prompt(layer, z, batch, width, lng, lnb, ws, bias, tri, lbl, gn):
    n_chunks = batch[1] // ROWS
    zcol, const = _mixer_specs(width, lambda a, c: a * n_chunks + c, 2)
    depth = lbl.shape[0]
    dk = width // HG_HEADS
    n_tok = batch[0] * batch[1]
    return pl.pallas_call(
        functools.partial(_mixer_prompt_kernel, layer),
        grid=(batch[0], n_chunks),
        in_specs=[zcol(j) for j in range(6)] + [
            const((1, width)), const((1, width)), const((SGU_GROUPS, ROWS, ROWS)),
            const((ROWS, width)), const((ROWS, ROWS)), const((depth, width)), const((1, width))],
        out_specs=[pl.BlockSpec((ROWS, width), lambda a, c: (a * n_chunks + c, 0)),
                   pl.BlockSpec((ROWS, width), lambda a, c: (a * n_chunks + c, 0)),
                   pl.BlockSpec((1, HG_HEADS, dk, dk), lambda a, c: (a, 0, 0, 0))],
        out_shape=[jax.ShapeDtypeStruct((n_tok, width), BF16),
                   jax.ShapeDtypeStruct((n_tok, width), BF16),
                   jax.ShapeDtypeStruct((batch[0], HG_HEADS, dk, dk), F32)],
        scratch_shapes=[pltpu.VMEM((HG_HEADS, dk, dk), F32), pltpu.VMEM((ROWS, width), F32)],
        compiler_params=_params(("parallel", "arbitrary"), 56),
        name="mixer_prompt",
    )(z, z, z, z, z, z, lng, lnb, ws, bias, tri, lbl, gn)


def _mixer_sample(layer, z, row0, batch, width, lng, lnb, ws, bias, tri, lbl, gn, s0):
    n_seq, seq_len = batch
    per_step = ROWS // seq_len
    steps = n_seq // per_step
    blk0 = row0 // ROWS
    zcol, const = _mixer_specs(width, lambda t, c: blk0 + t, 1)
    depth = lbl.shape[0]
    dk = width // HG_HEADS
    n_tok = n_seq * seq_len
    st_spec = pl.BlockSpec((per_step, HG_HEADS, dk, dk), lambda t: (t, 0, 0, 0))
    row_spec = pl.BlockSpec((ROWS, width), lambda t: (t, 0))
    return pl.pallas_call(
        functools.partial(_mixer_sample_kernel, layer, seq_len),
        grid=(steps,),
        in_specs=[zcol(j) for j in range(6)] + [
            const((1, width)), const((1, width)), const((SGU_GROUPS, ROWS, ROWS)),
            const((ROWS, width)), const((ROWS, ROWS)), const((depth, width)), const((1, width)),
            st_spec],
        out_specs=[row_spec, row_spec, row_spec, st_spec],
        out_shape=[jax.ShapeDtypeStruct((n_tok, width), BF16),
                   jax.ShapeDtypeStruct((n_tok, width), BF16),
                   jax.ShapeDtypeStruct((n_tok, width), F32),
                   jax.ShapeDtypeStruct((n_seq, HG_HEADS, dk, dk), F32)],
        scratch_shapes=[pltpu.VMEM((ROWS, width), F32), pltpu.VMEM((ROWS, dk), F32)],
        compiler_params=_params(("parallel",), 58),
        name="mixer_sample",
    )(z, z, z, z, z, z, lng, lnb, ws, bias, tri, lbl, gn, s0)


def _merge_kernel(alpha, ya_ref, yb_ref, ga_ref, gb_ref, x_ref, wpa_ref, wpb_ref, wo_ref, g_ref, b_ref,
                  x1_ref, x1t_ref):
    a = jnp.dot(ya_ref[...], wpa_ref[...], preferred_element_type=F32)
    b = jnp.dot(yb_ref[...], wpb_ref[...], preferred_element_type=F32)
    merged = _sigmoid(ga_ref[...]) * a + _sigmoid(gb_ref[...]) * b
    y = jnp.dot(merged.astype(BF16), wo_ref[...], preferred_element_type=F32)
    x1 = _layer_norm(alpha * x_ref[...] + y, g_ref[...], b_ref[...])
    x1_ref[...] = x1
    x1t_ref[...] = x1.T.astype(BF16)


def _merge(alpha, ya, yb, z, x, wpa, wpb, wo, g, b, tm, gate_blocks):
    n_tok, d = x.shape
    da = ya.shape[1]
    ga_blk, gb_blk = gate_blocks
    const = lambda shape: pl.BlockSpec(shape, lambda i: (0,) * len(shape), pipeline_mode=pl.Buffered(1))
    return pl.pallas_call(
        functools.partial(_merge_kernel, alpha),
        grid=(n_tok // tm,),
        in_specs=[pl.BlockSpec((tm, da), lambda i: (i, 0)),
                  pl.BlockSpec((tm, da), lambda i: (i, 0)),
                  pl.BlockSpec((tm, d), lambda i: (i, ga_blk)),
                  pl.BlockSpec((tm, d), lambda i: (i, gb_blk)),
                  pl.BlockSpec((tm, d), lambda i: (i, 0)),
                  const((da, d)), const((da, d)), const((d, d)), const((1, d)), const((1, d))],
        out_specs=[pl.BlockSpec((tm, d), lambda i: (i, 0)),
                   pl.BlockSpec((d, tm), lambda i: (0, i))],
        out_shape=[jax.ShapeDtypeStruct((n_tok, d), F32),
                   jax.ShapeDtypeStruct((d, n_tok), BF16)],
        compiler_params=_params(("parallel",), 58),
        name="merge_ln",
    )(ya, yb, z, z, x, wpa, wpb, wo, g, b)


def _compare_exchange(v, i, l):
    v[i], v[l] = jnp.maximum(v[i], v[l]), jnp.minimum(v[i], v[l])


def _bitonic_merge(v):
    n = len(v)
    j = n // 2
    while j:
        for i in range(n):
            if i ^ j > i:
                _compare_exchange(v, i, i ^ j)
        j //= 2
    return v


def _top_sorted(a):
    sub = 8
    n = a.shape[0] // sub
    v = [a[sub * k:sub * (k + 1)] for k in range(n)]
    k = 2
    while k <= n:
        j = k // 2
        while j:
            for i in range(n):
                l = i ^ j
                if l > i:
                    if i & k == 0:
                        _compare_exchange(v, i, l)
                    else:
                        _compare_exchange(v, l, i)
            j //= 2
        k *= 2
    shift = sub // 2
    while shift:
        other = [pltpu.roll(t, shift, 0) for t in v]
        v = _bitonic_merge([jnp.maximum(v[i], other[n - 1 - i]) for i in range(n)])
        shift //= 2
    return [t[0:1] for t in v]


def _stack_rows(rows, n):
    lanes = rows[0].shape[1]
    idx = lax.broadcasted_iota(jnp.int32, (n, lanes), 0)
    acc = jnp.broadcast_to(rows[0], (n, lanes))
    for r in range(1, n):
        acc = jnp.where(idx == r, rows[r], acc)
    return acc


def _peer_route_kernel(xt_ref, wqt_ref, sk_ref, thr_ref, c_ref, s2_ref, e2_ref, q_ref):
    q_ref[...] = jnp.dot(wqt_ref[...], xt_ref[...], preferred_element_type=F32).astype(BF16)
    half = sk_ref.shape[3]

    def per_head(h, carry):
        base = pl.multiple_of(h * 2 * half, 2 * half)
        s1 = jnp.dot(sk_ref[h, 0], q_ref[pl.ds(base, half), :], preferred_element_type=F32)
        s2 = jnp.dot(sk_ref[h, 1], q_ref[pl.ds(base + half, half), :], preferred_element_type=F32)
        t1 = _top_sorted(s1)
        t2 = _top_sorted(s2)
        t2_all = _stack_rows(t2, PEER_TOPK)
        t2_half = t2_all[:PEER_TOPK // 2]
        sums = [t1[a] + t2[b] for a in range(3, PEER_TOPK) for b in range(PEER_TOPK // (a + 1))]
        sums += [jnp.full_like(t1[0], -jnp.inf)] * (-len(sums) % 8)
        packed = [_stack_rows(sums[r:r + 8], 8) for r in range(0, len(sums), 8)]
        cand = jnp.concatenate([t1[0] + t2_all, t1[1] + t2_half, t1[2] + t2_half] + packed, axis=0)
        above = jnp.zeros_like(t1[0])
        tau = t1[0] + t2[0]
        for _ in range(PEER_TOPK):
            m = jnp.max(cand, axis=0, keepdims=True)
            eq = cand == m
            tau = jnp.where(above < PEER_TOPK, m, tau)
            above = above + jnp.sum(jnp.where(eq, 1.0, 0.0), axis=0, keepdims=True)
            cand = jnp.where(eq, -jnp.inf, cand)
        rows = [t1[a] + t2_all for a in range(PEER_TOPK)]
        count = lambda mask: jnp.sum(jnp.where(mask, 1.0, 0.0), axis=0, keepdims=True)
        n_gt = [count(r > tau) for r in rows]
        n_eq = [count(r == tau) for r in rows]
        spare = float(PEER_TOPK) - sum(n_gt)
        rank = lax.broadcasted_iota(jnp.int32, t2_all.shape, 0).astype(F32)
        top_score = t1[0] + t2[0]
        z = jnp.zeros_like(tau)
        thr = jnp.full(s1.shape, jnp.inf, F32)
        for a in range(PEER_TOPK):
            take = jnp.minimum(n_eq[a], spare)
            spare = spare - take
            keep = rank < n_gt[a] + take
            z = z + jnp.sum(jnp.where(keep, jnp.exp(rows[a] - top_score), 0.0), axis=0, keepdims=True)
            lowest_kept = jnp.min(jnp.where(keep, t2_all, jnp.inf), axis=0, keepdims=True)
            thr = jnp.where(s1 == t1[a], lowest_kept, thr)
        thr_ref[h] = thr
        c_ref[h] = jnp.exp(s1 - t1[0]) / z
        s2_ref[h] = s2
        e2_ref[h] = jnp.exp(s2 - t2[0])
        return carry

    lax.fori_loop(0, PEER_HEADS, per_head, 0)


def _peer_route(xt, wqt, sk, tb):
    d, n_tok = xt.shape
    nk = sk.shape[2]
    out = jax.ShapeDtypeStruct((PEER_HEADS, nk, n_tok), F32)
    ospec = pl.BlockSpec((PEER_HEADS, nk, tb), lambda i: (0, 0, i))
    return pl.pallas_call(
        _peer_route_kernel,
        grid=(n_tok // tb,),
        in_specs=[pl.BlockSpec((d, tb), lambda i: (0, i)),
                  pl.BlockSpec(wqt.shape, lambda i: (0, 0)),
                  pl.BlockSpec(sk.shape, lambda i: (0, 0, 0, 0))],
        out_specs=[ospec] * 4,
        out_shape=[out] * 4,
        scratch_shapes=[pltpu.VMEM((wqt.shape[0], tb), BF16)],
        compiler_params=_params(("parallel",), 48),
        name="peer_route",
    )(xt, wqt, sk)


def _peer_dense_kernel(n_tiles, n_exp_tiles, lane_chunk, xt_ref, u_ref, vt_ref, thr_ref, c_ref,
                       s2_ref, e2_ref, out_ref, ht_a, ht_b, pt_a, pt_b):
    s = pl.program_id(0)
    retrieved = jnp.clip(s - 2, 0, n_tiles - 1)

    @pl.when(s == 0)
    def _():
        for ref in (ht_a, ht_b, pt_a, pt_b):
            ref[...] = jnp.zeros_like(ref)

    @pl.when(retrieved % n_exp_tiles == 0)
    def _():
        out_ref[...] = jnp.zeros_like(out_ref)

    nk = s2_ref.shape[1]
    tb = xt_ref.shape[1]
    n_keys = u_ref.shape[0] // nk
    sub = 16

    def stages(ht_new, ht_old, pt_new, pt_old):
        def project(rs, ls):
            ht_new[rs, ls] = jnp.dot(u_ref[rs, :], xt_ref[:, ls], preferred_element_type=F32)

        def gate(keys, j0, ls):
            acc = {}
            for h in range(PEER_HEADS):
                s2 = s2_ref[h, j0:j0 + sub, ls]
                e2 = e2_ref[h, j0:j0 + sub, ls]
                for ii in keys:
                    hit = s2 >= thr_ref[h, ii:ii + 1, ls]
                    term = jnp.where(hit, e2, 0.0) * c_ref[h, ii:ii + 1, ls]
                    acc[ii] = term if h == 0 else acc[ii] + term
            for ii in keys:
                rs = slice(ii * nk + j0, ii * nk + j0 + sub)
                pt_new[rs, ls] = (acc[ii] * _gelu(ht_old[rs, ls])).astype(BF16)

        def retrieve(rs, ls):
            out_ref[rs, ls] += jnp.dot(vt_ref[rs, :], pt_old[:, ls], preferred_element_type=F32)

        lanes = [slice(t0, t0 + lane_chunk) for t0 in range(0, tb, lane_chunk)]
        blk = 256
        p_units = [functools.partial(project, slice(r0, r0 + blk), ls)
                   for ls in lanes for r0 in range(0, u_ref.shape[0], blk)]
        r_units = [functools.partial(retrieve, slice(r0, r0 + blk), ls)
                   for ls in lanes for r0 in range(0, vt_ref.shape[0], blk)]
        key_groups = [range(n_keys)]
        g_units = [functools.partial(gate, keys, j0, slice(g0, g0 + 128))
                   for g0 in range(0, tb, 128) for j0 in range(0, nk, sub) for keys in key_groups]
        for n, g_unit in enumerate(g_units):
            for units in (p_units, r_units):
                if n * len(units) % len(g_units) == 0:
                    units[n * len(units) // len(g_units)]()
            g_unit()

    @pl.when(s % 2 == 0)
    def _():
        stages(ht_a, ht_b, pt_b, pt_a)

    @pl.when(s % 2 == 1)
    def _():
        stages(ht_b, ht_a, pt_a, pt_b)


def _table_prep_kernel(u_ref, v_ref, ub_ref, vt_ref):
    ub_ref[...] = u_ref[...].astype(BF16)
    vt_ref[...] = v_ref[...].T.astype(BF16)


def _table_prep(peer_u, peer_v, eb):
    depth, n_exp, d = peer_u.shape
    in_spec = pl.BlockSpec((None, eb, d), lambda l, e: (l, e, 0))
    return pl.pallas_call(
        _table_prep_kernel,
        grid=(depth, n_exp // eb),
        in_specs=[in_spec, in_spec],
        out_specs=[pl.BlockSpec((None, eb, d), lambda l, e: (l, e, 0)),
                   pl.BlockSpec((None, d, eb), lambda l, e: (l, 0, e))],
        out_shape=[jax.ShapeDtypeStruct((depth, n_exp, d), BF16),
                   jax.ShapeDtypeStruct((depth, d, n_exp), BF16)],
        compiler_params=_params(("parallel", "parallel"), 48),
        name="table_prep",
    )(peer_u, peer_v)


def _peer_dense(layer, xt, u, vt, thr, c, s2, e2, tb, eb):
    d, n_tok = xt.shape
    n_exp = u.shape[1]
    nk = s2.shape[1]
    keys_per_blk = eb // nk
    n_e = n_exp // eb
    n_tiles = (n_tok // tb) * n_e
    proj = lambda s: jnp.minimum(s, n_tiles - 1)
    gate = lambda s: jnp.clip(s - 1, 0, n_tiles - 1)
    retr = lambda s: jnp.clip(s - 2, 0, n_tiles - 1)
    row_spec = pl.BlockSpec((PEER_HEADS, keys_per_blk, tb), lambda s: (0, gate(s) % n_e, gate(s) // n_e))
    all_spec = pl.BlockSpec((PEER_HEADS, nk, tb), lambda s: (0, 0, gate(s) // n_e))
    return pl.pallas_call(
        functools.partial(_peer_dense_kernel, n_tiles, n_e, 256),
        grid=(n_tiles + 2,),
        in_specs=[pl.BlockSpec((d, tb), lambda s: (0, proj(s) // n_e)),
                  pl.BlockSpec((None, eb, d), lambda s: (layer, proj(s) % n_e, 0)),
                  pl.BlockSpec((None, d, eb), lambda s: (layer, 0, retr(s) % n_e)),
                  row_spec, row_spec, all_spec, all_spec],
        out_specs=pl.BlockSpec((d, tb), lambda s: (0, retr(s) // n_e)),
        out_shape=jax.ShapeDtypeStruct((d, n_tok), F32),
        scratch_shapes=[pltpu.VMEM((eb, tb), F32), pltpu.VMEM((eb, tb), F32),
                        pltpu.VMEM((eb, tb), BF16), pltpu.VMEM((eb, tb), BF16)],
        compiler_params=_params(("arbitrary",), 58),
        name="peer_dense",
    )(xt, u, vt, thr, c, s2, e2)


def _residual_ln_kernel(alpha, x_ref, pt_ref, g_ref, b_ref, o_ref, ob_ref):
    y = _layer_norm(alpha * x_ref[...] + pt_ref[...].T, g_ref[...], b_ref[...])
    o_ref[...] = y
    ob_ref[...] = y.astype(BF16)


def _residual_ln(alpha, x, pt, g, b, tm, row0=0, n_tok=None):
    d = x.shape[1]
    n_tok = x.shape[0] if n_tok is None else n_tok
    blk0 = row0 // tm
    return pl.pallas_call(
        functools.partial(_residual_ln_kernel, alpha),
        grid=(n_tok // tm,),
        in_specs=[pl.BlockSpec((tm, d), lambda i: (blk0 + i, 0)),
                  pl.BlockSpec((d, tm), lambda i: (0, blk0 + i)),
                  pl.BlockSpec((1, d), lambda i: (0, 0)),
                  pl.BlockSpec((1, d), lambda i: (0, 0))],
        out_specs=[pl.BlockSpec((tm, d), lambda i: (i, 0))] * 2,
        out_shape=[jax.ShapeDtypeStruct((n_tok, d), F32), jax.ShapeDtypeStruct((n_tok, d), BF16)],
        compiler_params=_params(("parallel",), 40),
        name="residual_ln",
    )(x, pt, g, b)


def _block_diag(w, reps):
    return jnp.kron(jnp.eye(reps, dtype=w.dtype), w)


def kernel(x_prompt, x_sample, state_hgrn, w_in, sgu_ln_g, sgu_ln_b, sgu_w, sgu_b, hg_lb_logits,
           hg_norm_g, w_branch_a, w_branch_b, w_out, ln1_g, ln1_b, peer_w_q, peer_sub_keys,
           peer_u, peer_v, ln2_g, ln2_b):
    depth = w_in.shape[0]
    n_b, seq, d = x_prompt.shape
    n_s, dec_seq, _ = x_sample.shape
    width = sgu_ln_g.shape[1]
    n_prompt = n_b * seq
    alpha = (2 * depth) ** 0.25
    assert seq % ROWS == 0 and ROWS % dec_seq == 0 and n_s % (ROWS // dec_seq) == 0
    assert w_in.shape[2] == 6 * width + 2 * d and d == 2 * width

    x = jnp.concatenate([x_prompt.reshape(n_prompt, d), x_sample.reshape(n_s * dec_seq, d)], axis=0)
    xb = x.astype(BF16)
    row = lambda a: a.reshape(1, -1)

    tri = jnp.tril(jnp.ones((ROWS, ROWS), F32))
    tri_s = _block_diag(jnp.tril(jnp.ones((dec_seq, dec_seq), F32)), ROWS // dec_seq)

    u_tab, vt_tab = _table_prep(peer_u, peer_v, 512)

    new_sp, new_ss, new_v = [], [], []
    for l in range(depth):
        z = _matmul(xb, w_in[l].astype(BF16), 1024, 1024, "in_proj")

        ws = jnp.where(tri > 0, sgu_w[l], 0.0)
        ws_s = jax.vmap(lambda w: _block_diag(w, ROWS // dec_seq))(ws[:, :dec_seq, :dec_seq])
        bias = jnp.repeat(sgu_b[l].T, width // SGU_GROUPS, axis=1)
        bias_s = jnp.tile(bias[:dec_seq], (ROWS // dec_seq, 1))
        common = (row(sgu_ln_g[l]), row(sgu_ln_b[l]))
        tail = (hg_lb_logits, row(hg_norm_g[l]))
        ya_p, yb_p, st_p = _mixer_prompt(l, z, (n_b, seq), width, *common, ws.astype(BF16), bias,
                                         tri.astype(BF16), *tail)
        ya_s, yb_s, vn_s, st_s = _mixer_sample(l, z, n_prompt, (n_s, dec_seq), width, *common,
                                               ws_s.astype(BF16), bias_s, tri_s.astype(BF16), *tail,
                                               state_hgrn[l])
        ya = jnp.concatenate([ya_p, ya_s], axis=0)
        yb = jnp.concatenate([yb_p, yb_s], axis=0)

        x1, x1t = _merge(alpha, ya, yb, z, x, w_branch_a[l].astype(BF16), w_branch_b[l].astype(BF16),
                         w_out[l].astype(BF16), row(ln1_g[l]), row(ln1_b[l]), 256,
                         (6 * width // d, 6 * width // d + 1))

        thr, c, s2, e2 = _peer_route(x1t, peer_w_q[l].T.astype(BF16), peer_sub_keys[l].astype(BF16), 256)
        pt = _peer_dense(l, x1t, u_tab, vt_tab, thr, c, s2, e2, 512, 1024)
        ln2 = (row(ln2_g[l]), row(ln2_b[l]), 512)
        if l + 1 < depth:
            x, xb = _residual_ln(alpha, x1, pt, *ln2)
        else:
            y_p, _ = _residual_ln(alpha, x1, pt, *ln2, 0, n_prompt)
            y_s, _ = _residual_ln(alpha, x1, pt, *ln2, n_prompt, n_s * dec_seq)

        new_sp.append(st_p)
        new_ss.append(st_s)
        new_v.append(vn_s.reshape(n_s, dec_seq, width))

    return (y_p.reshape(n_b, seq, d), y_s.reshape(n_s, dec_seq, d),
            jnp.stack(new_sp), jnp.stack(new_ss), jnp.stack(new_v))
```

```python
import functools

import jax
import jax.numpy as jnp
from jax import lax
from jax.experimental import pallas as pl
from jax.experimental.pallas import tpu as pltpu

F32 = jnp.float32
BF16 = jnp.bfloat16

SGU_GROUPS = 4
HG_HEADS = 8
PEER_HEADS = 8
PEER_TOPK = 16
LN_EPS = 1e-5
RMS_EPS = 1e-6
ROWS = 128
HG_SUB = 16
MIB = 1024 * 1024

_NT = (((1,), (1,)), ((), ()))
_TN = (((0,), (0,)), ((), ()))


def _params(sem, vmem_mib):
    return pltpu.CompilerParams(dimension_semantics=sem, vmem_limit_bytes=vmem_mib * MIB)


def _gelu(x):
    c = 0.7978845608028654
    half = 0.5 * x
    return half + half * jnp.tanh(x * (c + (c * 0.044715) * (x * x)))


def _sigmoid(x):
    return 1.0 / (1.0 + jnp.exp(-x))


def _layer_norm(x, g, b):
    mu = jnp.mean(x, axis=-1, keepdims=True)
    xc = x - mu
    var = jnp.mean(xc * xc, axis=-1, keepdims=True)
    return xc * lax.rsqrt(var + LN_EPS) * g + b


def _bdot(a, b, dims=None):
    a = a.astype(BF16)
    b = b.astype(BF16)
    if dims is None:
        return jnp.dot(a, b, preferred_element_type=F32)
    return lax.dot_general(a, b, dims, preferred_element_type=F32)


def _matmul_kernel(x_ref, w_ref, o_ref, wb_ref):
    @pl.when(pl.program_id(1) == 0)
    def _():
        wb_ref[...] = w_ref[...].astype(BF16)

    o_ref[...] = jnp.dot(x_ref[...], wb_ref[...], preferred_element_type=F32)


def _matmul(x, w, layer, tm, tn, name):
    m, k = x.shape
    n = w.shape[2]
    return pl.pallas_call(
        _matmul_kernel,
        grid=(n // tn, m // tm),
        in_specs=[pl.BlockSpec((tm, k), lambda j, i: (i, 0)),
                  pl.BlockSpec((None, k, tn), lambda j, i: (layer, 0, j))],
        out_specs=pl.BlockSpec((tm, tn), lambda j, i: (i, j)),
        out_shape=jax.ShapeDtypeStruct((m, n), F32),
        scratch_shapes=[pltpu.VMEM((k, tn), BF16)],
        compiler_params=_params(("parallel", "arbitrary"), 48),
        name=name,
    )(x, w)


def _forget_lower_bound(logits_ref, layer):
    lg = logits_ref[...]
    e = jnp.exp(lg - jnp.max(lg, axis=0, keepdims=True))
    p = e / jnp.sum(e, axis=0, keepdims=True)
    return jnp.sum(p[1:layer + 1], axis=0, keepdims=True)


def _hgrn_gates(fp, lb):
    e = jnp.exp(-jnp.abs(fp))
    log_sig = jnp.minimum(fp, 0.0) - jnp.log1p(e)
    sig_neg = jnp.where(fp >= 0.0, e, 1.0) / (1.0 + e)
    if lb is None:
        return log_sig, sig_neg
    k = (1.0 - lb) * sig_neg
    pos = lb > 0.0
    a = jnp.log(jnp.where(pos, lb, 1.0))
    b = jnp.log1p(-lb) + log_sig
    lse = jnp.maximum(a, b) + jnp.log1p(jnp.exp(-jnp.abs(a - b)))
    return jnp.where(pos, lse, b), k


def _segment_cumsum(t_ref, x):
    t = t_ref[...]
    hi = x.astype(BF16)
    r1 = x - hi.astype(F32)
    mid = r1.astype(BF16)
    lo = (r1 - mid.astype(F32)).astype(BF16)
    dot = functools.partial(jnp.dot, preferred_element_type=F32)
    return dot(t, hi) + dot(t, mid) + dot(t, lo)


def _sgu(u, v, lng_ref, lnb_ref, ws_ref, bias_ref, ya_ref):
    vn = _layer_norm(_gelu(v), lng_ref[...], lnb_ref[...])
    vnb = vn.astype(BF16)
    gw = vn.shape[1] // SGU_GROUPS
    for g in range(SGU_GROUPS):
        cs = slice(g * gw, (g + 1) * gw)
        mixed = jnp.dot(ws_ref[g], vnb[:, cs], preferred_element_type=F32) + bias_ref[:, cs]
        ya_ref[:, cs] = (_gelu(u[:, cs]) * mixed).astype(BF16)
    return vn


def _hgrn_out(o, gn, og):
    o = o * lax.rsqrt(jnp.mean(o * o, axis=-1, keepdims=True) + RMS_EPS)
    return (o * gn * (og * _sigmoid(og))).astype(BF16)


def _rows_at(b_ref, size, offset, width):
    parts = []
    for j in range(ROWS // size):
        r = j * size + offset
        if r < 0:
            parts.append(jnp.zeros((size, width), F32))
        else:
            parts.append(jnp.broadcast_to(b_ref[pl.ds(r, 1), :], (size, width)))
    return parts[0] if len(parts) == 1 else jnp.concatenate(parts, axis=0)


def _mixer_prompt_kernel(layer, u_ref, v_ref, q_ref, f_ref, i_ref, og_ref, lng_ref, lnb_ref,
                         ws_ref, bias_ref, tri_ref, lbl_ref, gn_ref,
                         ya_ref, yb_ref, st_ref, stt_ref, b_ref):
    c = pl.program_id(1)
    width = q_ref.shape[1]
    dk = width // HG_HEADS

    @pl.when(c == 0)
    def _():
        stt_ref[...] = jnp.zeros_like(stt_ref)

    _sgu(u_ref[...], v_ref[...], lng_ref, lnb_ref, ws_ref, bias_ref, ya_ref)

    lb = None if layer == 0 else _forget_lower_bound(lbl_ref, layer)
    lf, k = _hgrn_gates(f_ref[...], lb)
    qs = q_ref[...] * (dk ** -0.5)
    b = _segment_cumsum(tri_ref, lf)
    b_ref[...] = b

    e_sub = b - _rows_at(b_ref, HG_SUB, -1, width)
    q_lvl = [qs * jnp.exp(e_sub)]
    k_lvl = [k * jnp.exp(-e_sub)]
    size = HG_SUB
    sizes = []
    while size < ROWS:
        sizes.append(size)
        q_lvl.append(q_lvl[0] if size == HG_SUB else qs * jnp.exp(b - _rows_at(b_ref, size, -1, width)))
        k_lvl.append(k * jnp.exp(_rows_at(b_ref, size, size - 1, width) - b))
        size *= 2
    qd = qs * jnp.exp(b)
    b_end = b_ref[pl.ds(ROWS - 1, 1), :]
    ke = k * jnp.exp(b_end - b)
    g_end = jnp.exp(b_end)

    row = lax.broadcasted_iota(jnp.int32, (ROWS, ROWS), 0)
    col = lax.broadcasted_iota(jnp.int32, (ROWS, ROWS), 1)
    masks = [(row // HG_SUB == col // HG_SUB) & (col <= row)]
    for sz in sizes:
        masks.append(((row // sz) % 2 == 1) & (col // sz == row // sz - 1))

    vv = i_ref[...]
    gn = gn_ref[...]
    og = og_ref[...]
    for h in range(HG_HEADS):
        hs = slice(h * dk, (h + 1) * dk)
        attn = jnp.zeros((ROWS, ROWS), F32)
        for ql, kl, mk in zip(q_lvl, k_lvl, masks):
            attn = jnp.where(mk, _bdot(ql[:, hs], kl[:, hs], _NT), attn)
        vh = vv[:, hs]
        stt = stt_ref[h]
        o = _bdot(attn, vh) + _bdot(qd[:, hs], stt, _NT)
        stt_ref[h] = stt * g_end[:, hs] + _bdot(vh, ke[:, hs], _TN)
        yb_ref[:, hs] = _hgrn_out(o, gn[:, hs], og[:, hs])

    @pl.when(c == pl.num_programs(1) - 1)
    def _():
        for h in range(HG_HEADS):
            st_ref[0, h] = stt_ref[h].T


def _mixer_sample_kernel(layer, seq_len, u_ref, v_ref, q_ref, f_ref, i_ref, og_ref, lng_ref, lnb_ref,
                         ws_ref, bias_ref, tri_ref, lbl_ref, gn_ref, s0_ref,
                         ya_ref, yb_ref, vn_ref, st_ref, b_ref, oi_ref):
    width = q_ref.shape[1]
    dk = width // HG_HEADS
    n_seq = ROWS // seq_len

    vn_ref[...] = _sgu(u_ref[...], v_ref[...], lng_ref, lnb_ref, ws_ref, bias_ref, ya_ref)

    lb = None if layer == 0 else _forget_lower_bound(lbl_ref, layer)
    lf, k = _hgrn_gates(f_ref[...], lb)
    qs = q_ref[...] * (dk ** -0.5)
    b = _segment_cumsum(tri_ref, lf)
    b_ref[...] = b
    qd = qs * jnp.exp(b)
    k0 = k * jnp.exp(-b)
    ke = k * jnp.exp(_rows_at(b_ref, seq_len, seq_len - 1, width) - b)

    row = lax.broadcasted_iota(jnp.int32, (ROWS, ROWS), 0)
    col = lax.broadcasted_iota(jnp.int32, (ROWS, ROWS), 1)
    causal = (row // seq_len == col // seq_len) & (col <= row)
    seq_of_row = lax.broadcasted_iota(jnp.int32, (ROWS, dk), 0) // seq_len

    vv = i_ref[...]
    gn = gn_ref[...]
    og = og_ref[...]
    for h in range(HG_HEADS):
        hs = slice(h * dk, (h + 1) * dk)
        attn = jnp.where(causal, _bdot(qd[:, hs], k0[:, hs], _NT), 0.0)
        vh = vv[:, hs]
        qdh = qd[:, hs].astype(BF16)
        keh = ke[:, hs].astype(BF16)
        oi_ref[...] = _bdot(attn, vh)

        def per_seq(s, carry, h=h, hs=hs, vh=vh, qdh=qdh, keh=keh):
            mine = seq_of_row == s
            s0 = s0_ref[s, h]
            oi_ref[...] += jnp.where(mine, _bdot(qdh, s0), 0.0)
            b_seq = b_ref[pl.ds(pl.multiple_of(s * seq_len, seq_len), seq_len), hs]
            g_end = jnp.exp(b_seq[seq_len - 1:seq_len])
            new_t = s0.T * g_end + _bdot(jnp.where(mine, vh, 0.0), keh, _TN)
            st_ref[s, h] = new_t.T
            return carry

        lax.fori_loop(0, n_seq, per_seq, 0, unroll=4)
        yb_ref[:, hs] = _hgrn_out(oi_ref[...], gn[:, hs], og[:, hs])


def _mixer_specs(width, row_block, n_axes):
    def idx(f):
        return f if n_axes == 2 else (lambda t: f(t, 0))

    def zcol(j):
        return pl.BlockSpec((ROWS, width), idx(lambda a, c, j=j: (row_block(a, c), j)))

    def const(shape):
        return pl.BlockSpec(shape, idx(lambda a, c: (0,) * len(shape)))

    return zcol, const


def _mixer_prompt(layer, z, batch, width, lng, lnb, ws, bias, tri, lbl, gn):
    n_chunks = batch[1] // ROWS
    zcol, const = _mixer_specs(width, lambda a, c: a * n_chunks + c, 2)
    depth = lbl.shape[0]
    dk = width // HG_HEADS
    n_tok = batch[0] * batch[1]
    return pl.pallas_call(
        functools.partial(_mixer_prompt_kernel, layer),
        grid=(batch[0], n_chunks),
        in_specs=[zcol(j) for j in range(6)] + [
            const((1, width)), const((1, width)), const((SGU_GROUPS, ROWS, ROWS)),
            const((ROWS, width)), const((ROWS, ROWS)), const((depth, width)), const((1, width))],
        out_specs=[pl.BlockSpec((ROWS, width), lambda a, c: (a * n_chunks + c, 0)),
                   pl.BlockSpec((ROWS, width), lambda a, c: (a * n_chunks + c, 0)),
                   pl.BlockSpec((1, HG_HEADS, dk, dk), lambda a, c: (a, 0, 0, 0))],
        out_shape=[jax.ShapeDtypeStruct((n_tok, width), BF16),
                   jax.ShapeDtypeStruct((n_tok, width), BF16),
                   jax.ShapeDtypeStruct((batch[0], HG_HEADS, dk, dk), F32)],
        scratch_shapes=[pltpu.VMEM((HG_HEADS, dk, dk), F32), pltpu.VMEM((ROWS, width), F32)],
        compiler_params=_params(("parallel", "arbitrary"), 56),
        name="mixer_prompt",
    )(z, z, z, z, z, z, lng, lnb, ws, bias, tri, lbl, gn)


def _mixer_sample(layer, z, row0, batch, width, lng, lnb, ws, bias, tri, lbl, gn, s0):
    n_seq, seq_len = batch
    per_step = ROWS // seq_len
    steps = n_seq // per_step
    blk0 = row0 // ROWS
    zcol, const = _mixer_specs(width, lambda t, c: blk0 + t, 1)
    depth = lbl.shape[0]
    dk = width // HG_HEADS
    n_tok = n_seq * seq_len
    st_spec = pl.BlockSpec((per_step, HG_HEADS, dk, dk), lambda t: (t, 0, 0, 0))
    row_spec = pl.BlockSpec((ROWS, width), lambda t: (t, 0))
    return pl.pallas_call(
        functools.partial(_mixer_sample_kernel, layer, seq_len),
        grid=(steps,),
        in_specs=[zcol(j) for j in range(6)] + [
            const((1, width)), const((1, width)), const((SGU_GROUPS, ROWS, ROWS)),
            const((ROWS, width)), const((ROWS, ROWS)), const((depth, width)), const((1, width)),
            pl.BlockSpec((None, per_step, HG_HEADS, dk, dk), lambda t: (layer, t, 0, 0, 0))],
        out_specs=[row_spec, row_spec, row_spec, st_spec],
        out_shape=[jax.ShapeDtypeStruct((n_tok, width), BF16),
                   jax.ShapeDtypeStruct((n_tok, width), BF16),
                   jax.ShapeDtypeStruct((n_tok, width), F32),
                   jax.ShapeDtypeStruct((n_seq, HG_HEADS, dk, dk), F32)],
        scratch_shapes=[pltpu.VMEM((ROWS, width), F32), pltpu.VMEM((ROWS, dk), F32)],
        compiler_params=_params(("parallel",), 58),
        name="mixer_sample",
    )(z, z, z, z, z, z, lng, lnb, ws, bias, tri, lbl, gn, s0)


def _merge_kernel(alpha, ya_ref, yb_ref, ga_ref, gb_ref, x_ref, wpa_ref, wpb_ref, wo_ref, g_ref, b_ref,
                  x1_ref, x1t_ref):
    a = jnp.dot(ya_ref[...], wpa_ref[...], preferred_element_type=F32)
    b = jnp.dot(yb_ref[...], wpb_ref[...], preferred_element_type=F32)
    merged = _sigmoid(ga_ref[...]) * a + _sigmoid(gb_ref[...]) * b
    y = jnp.dot(merged.astype(BF16), wo_ref[...], preferred_element_type=F32)
    x1 = _layer_norm(alpha * x_ref[...] + y, g_ref[...], b_ref[...])
    x1_ref[...] = x1
    x1t_ref[...] = x1.T.astype(BF16)


def _merge(alpha, ya, yb, z, x, wpa, wpb, wo, g, b, tm, gate_blocks):
    n_tok, d = x.shape
    da = ya.shape[1]
    ga_blk, gb_blk = gate_blocks
    const = lambda shape: pl.BlockSpec(shape, lambda i: (0,) * len(shape), pipeline_mode=pl.Buffered(1))
    return pl.pallas_call(
        functools.partial(_merge_kernel, alpha),
        grid=(n_tok // tm,),
        in_specs=[pl.BlockSpec((tm, da), lambda i: (i, 0)),
                  pl.BlockSpec((tm, da), lambda i: (i, 0)),
                  pl.BlockSpec((tm, d), lambda i: (i, ga_blk)),
                  pl.BlockSpec((tm, d), lambda i: (i, gb_blk)),
                  pl.BlockSpec((tm, d), lambda i: (i, 0)),
                  const((da, d)), const((da, d)), const((d, d)), const((1, d)), const((1, d))],
        out_specs=[pl.BlockSpec((tm, d), lambda i: (i, 0)),
                   pl.BlockSpec((d, tm), lambda i: (0, i))],
        out_shape=[jax.ShapeDtypeStruct((n_tok, d), F32),
                   jax.ShapeDtypeStruct((d, n_tok), BF16)],
        compiler_params=_params(("parallel",), 58),
        name="merge_ln",
    )(ya, yb, z, z, x, wpa, wpb, wo, g, b)


def _compare_exchange(v, i, l):
    v[i], v[l] = jnp.maximum(v[i], v[l]), jnp.minimum(v[i], v[l])


def _bitonic_merge(v):
    n = len(v)
    j = n // 2
    while j:
        for i in range(n):
            if i ^ j > i:
                _compare_exchange(v, i, i ^ j)
        j //= 2
    return v


def _top_sorted(a):
    sub = 8
    n = a.shape[0] // sub
    v = [a[sub * k:sub * (k + 1)] for k in range(n)]
    k = 2
    while k <= n:
        j = k // 2
        while j:
            for i in range(n):
                l = i ^ j
                if l > i:
                    if i & k == 0:
                        _compare_exchange(v, i, l)
                    else:
                        _compare_exchange(v, l, i)
            j //= 2
        k *= 2
    shift = sub // 2
    while shift:
        other = [pltpu.roll(t, shift, 0) for t in v]
        v = _bitonic_merge([jnp.maximum(v[i], other[n - 1 - i]) for i in range(n)])
        shift //= 2
    return [t[0:1] for t in v]


def _stack_rows(rows, n):
    lanes = rows[0].shape[1]
    idx = lax.broadcasted_iota(jnp.int32, (n, lanes), 0)
    acc = jnp.broadcast_to(rows[0], (n, lanes))
    for r in range(1, n):
        acc = jnp.where(idx == r, rows[r], acc)
    return acc


def _peer_route_kernel(xt_ref, wqt_ref, sk_ref, thr_ref, c_ref, s2_ref, e2_ref, q_ref):
    q_ref[...] = jnp.dot(wqt_ref[...], xt_ref[...], preferred_element_type=F32).astype(BF16)
    half = sk_ref.shape[3]

    def per_head(h, carry):
        base = pl.multiple_of(h * 2 * half, 2 * half)
        s1 = jnp.dot(sk_ref[h, 0], q_ref[pl.ds(base, half), :], preferred_element_type=F32)
        s2 = jnp.dot(sk_ref[h, 1], q_ref[pl.ds(base + half, half), :], preferred_element_type=F32)
        t1 = _top_sorted(s1)
        t2 = _top_sorted(s2)
        t2_all = _stack_rows(t2, PEER_TOPK)
        t2_half = t2_all[:PEER_TOPK // 2]
        sums = [t1[a] + t2[b] for a in range(3, PEER_TOPK) for b in range(PEER_TOPK // (a + 1))]
        sums += [jnp.full_like(t1[0], -jnp.inf)] * (-len(sums) % 8)
        packed = [_stack_rows(sums[r:r + 8], 8) for r in range(0, len(sums), 8)]
        cand = jnp.concatenate([t1[0] + t2_all, t1[1] + t2_half, t1[2] + t2_half] + packed, axis=0)
        above = jnp.zeros_like(t1[0])
        tau = t1[0] + t2[0]
        for _ in range(PEER_TOPK):
            m = jnp.max(cand, axis=0, keepdims=True)
            eq = cand == m
            tau = jnp.where(above < PEER_TOPK, m, tau)
            above = above + jnp.sum(jnp.where(eq, 1.0, 0.0), axis=0, keepdims=True)
            cand = jnp.where(eq, -jnp.inf, cand)
        rows = [t1[a] + t2_all for a in range(PEER_TOPK)]
        count = lambda mask: jnp.sum(jnp.where(mask, 1.0, 0.0), axis=0, keepdims=True)
        n_gt = [count(r > tau) for r in rows]
        n_eq = [count(r == tau) for r in rows]
        spare = float(PEER_TOPK) - sum(n_gt)
        rank = lax.broadcasted_iota(jnp.int32, t2_all.shape, 0).astype(F32)
        top_score = t1[0] + t2[0]
        z = jnp.zeros_like(tau)
        thr = jnp.full(s1.shape, jnp.inf, F32)
        for a in range(PEER_TOPK):
            take = jnp.minimum(n_eq[a], spare)
            spare = spare - take
            keep = rank < n_gt[a] + take
            z = z + jnp.sum(jnp.where(keep, jnp.exp(rows[a] - top_score), 0.0), axis=0, keepdims=True)
            lowest_kept = jnp.min(jnp.where(keep, t2_all, jnp.inf), axis=0, keepdims=True)
            thr = jnp.where(s1 == t1[a], lowest_kept, thr)
        thr_ref[h] = thr
        c_ref[h] = jnp.exp(s1 - t1[0]) / z
        s2_ref[h] = s2
        e2_ref[h] = jnp.exp(s2 - t2[0])
        return carry

    lax.fori_loop(0, PEER_HEADS, per_head, 0)


def _peer_route(xt, wqt, sk, tb):
    d, n_tok = xt.shape
    nk = sk.shape[2]
    out = jax.ShapeDtypeStruct((PEER_HEADS, nk, n_tok), F32)
    ospec = pl.BlockSpec((PEER_HEADS, nk, tb), lambda i: (0, 0, i))
    return pl.pallas_call(
        _peer_route_kernel,
        grid=(n_tok // tb,),
        in_specs=[pl.BlockSpec((d, tb), lambda i: (0, i)),
                  pl.BlockSpec(wqt.shape, lambda i: (0, 0)),
                  pl.BlockSpec(sk.shape, lambda i: (0, 0, 0, 0))],
        out_specs=[ospec] * 4,
        out_shape=[out] * 4,
        scratch_shapes=[pltpu.VMEM((wqt.shape[0], tb), BF16)],
        compiler_params=_params(("parallel",), 48),
        name="peer_route",
    )(xt, wqt, sk)


def _peer_dense_kernel(n_tiles, n_exp_tiles, lane_chunk, xt_ref, u_ref, vt_ref, thr_ref, c_ref,
                       s2_ref, e2_ref, out_ref, ht_a, ht_b, pt_a, pt_b):
    s = pl.program_id(0)
    retrieved = jnp.clip(s - 2, 0, n_tiles - 1)

    @pl.when(s == 0)
    def _():
        for ref in (ht_a, ht_b, pt_a, pt_b):
            ref[...] = jnp.zeros_like(ref)

    @pl.when(retrieved % n_exp_tiles == 0)
    def _():
        out_ref[...] = jnp.zeros_like(out_ref)

    nk = s2_ref.shape[1]
    tb = xt_ref.shape[1]
    n_keys = u_ref.shape[0] // nk
    sub = 16

    def stages(ht_new, ht_old, pt_new, pt_old):
        def project(rs, ls):
            ht_new[rs, ls] = jnp.dot(u_ref[rs, :], xt_ref[:, ls], preferred_element_type=F32)

        def gate(keys, j0, ls):
            acc = {}
            for h in range(PEER_HEADS):
                s2 = s2_ref[h, j0:j0 + sub, ls]
                e2 = e2_ref[h, j0:j0 + sub, ls]
                for ii in keys:
                    hit = s2 >= thr_ref[h, ii:ii + 1, ls]
                    term = jnp.where(hit, e2, 0.0) * c_ref[h, ii:ii + 1, ls]
                    acc[ii] = term if h == 0 else acc[ii] + term
            for ii in keys:
                rs = slice(ii * nk + j0, ii * nk + j0 + sub)
                pt_new[rs, ls] = (acc[ii] * _gelu(ht_old[rs, ls])).astype(BF16)

        def retrieve(rs, ls):
            out_ref[rs, ls] += jnp.dot(vt_ref[rs, :], pt_old[:, ls], preferred_element_type=F32)

        lanes = [slice(t0, t0 + lane_chunk) for t0 in range(0, tb, lane_chunk)]
        blk = 256
        p_units = [functools.partial(project, slice(r0, r0 + blk), ls)
                   for ls in lanes for r0 in range(0, u_ref.shape[0], blk)]
        r_units = [functools.partial(retrieve, slice(r0, r0 + blk), ls)
                   for ls in lanes for r0 in range(0, vt_ref.shape[0], blk)]
        key_groups = [range(n_keys)]
        g_units = [functools.partial(gate, keys, j0, slice(g0, g0 + 128))
                   for g0 in range(0, tb, 128) for j0 in range(0, nk, sub) for keys in key_groups]
        for n, g_unit in enumerate(g_units):
            for units in (p_units, r_units):
                if n * len(units) % len(g_units) == 0:
                    units[n * len(units) // len(g_units)]()
            g_unit()

    @pl.when(s % 2 == 0)
    def _():
        stages(ht_a, ht_b, pt_b, pt_a)

    @pl.when(s % 2 == 1)
    def _():
        stages(ht_b, ht_a, pt_a, pt_b)


def _table_prep_kernel(u_ref, v_ref, ub_ref, vt_ref):
    ub_ref[...] = u_ref[...].astype(BF16)
    vt_ref[...] = v_ref[...].T.astype(BF16)


def _table_prep(peer_u, peer_v, eb):
    depth, n_exp, d = peer_u.shape
    in_spec = pl.BlockSpec((None, eb, d), lambda l, e: (l, e, 0))
    return pl.pallas_call(
        _table_prep_kernel,
        grid=(depth, n_exp // eb),
        in_specs=[in_spec, in_spec],
        out_specs=[pl.BlockSpec((None, eb, d), lambda l, e: (l, e, 0)),
                   pl.BlockSpec((None, d, eb), lambda l, e: (l, 0, e))],
        out_shape=[jax.ShapeDtypeStruct((depth, n_exp, d), BF16),
                   jax.ShapeDtypeStruct((depth, d, n_exp), BF16)],
        compiler_params=_params(("parallel", "parallel"), 48),
        name="table_prep",
    )(peer_u, peer_v)


def _peer_dense(layer, xt, u, vt, thr, c, s2, e2, tb, eb):
    d, n_tok = xt.shape
    n_exp = u.shape[1]
    nk = s2.shape[1]
    keys_per_blk = eb // nk
    n_e = n_exp // eb
    n_tiles = (n_tok // tb) * n_e
    proj = lambda s: jnp.minimum(s, n_tiles - 1)
    gate = lambda s: jnp.clip(s - 1, 0, n_tiles - 1)
    retr = lambda s: jnp.clip(s - 2, 0, n_tiles - 1)
    row_spec = pl.BlockSpec((PEER_HEADS, keys_per_blk, tb), lambda s: (0, gate(s) % n_e, gate(s) // n_e))
    all_spec = pl.BlockSpec((PEER_HEADS, nk, tb), lambda s: (0, 0, gate(s) // n_e))
    return pl.pallas_call(
        functools.partial(_peer_dense_kernel, n_tiles, n_e, 256),
        grid=(n_tiles + 2,),
        in_specs=[pl.BlockSpec((d, tb), lambda s: (0, proj(s) // n_e)),
                  pl.BlockSpec((None, eb, d), lambda s: (layer, proj(s) % n_e, 0)),
                  pl.BlockSpec((None, d, eb), lambda s: (layer, 0, retr(s) % n_e)),
                  row_spec, row_spec, all_spec, all_spec],
        out_specs=pl.BlockSpec((d, tb), lambda s: (0, retr(s) // n_e)),
        out_shape=jax.ShapeDtypeStruct((d, n_tok), F32),
        scratch_shapes=[pltpu.VMEM((eb, tb), F32), pltpu.VMEM((eb, tb), F32),
                        pltpu.VMEM((eb, tb), BF16), pltpu.VMEM((eb, tb), BF16)],
        compiler_params=_params(("arbitrary",), 58),
        name="peer_dense",
    )(xt, u, vt, thr, c, s2, e2)


def _residual_ln_kernel(alpha, x_ref, pt_ref, g_ref, b_ref, o_ref, ob_ref):
    y = _layer_norm(alpha * x_ref[...] + pt_ref[...].T, g_ref[...], b_ref[...])
    o_ref[...] = y
    ob_ref[...] = y.astype(BF16)


def _residual_ln(alpha, x, pt, g, b, tm, row0=0, n_tok=None):
    d = x.shape[1]
    n_tok = x.shape[0] if n_tok is None else n_tok
    blk0 = row0 // tm
    return pl.pallas_call(
        functools.partial(_residual_ln_kernel, alpha),
        grid=(n_tok // tm,),
        in_specs=[pl.BlockSpec((tm, d), lambda i: (blk0 + i, 0)),
                  pl.BlockSpec((d, tm), lambda i: (0, blk0 + i)),
                  pl.BlockSpec((1, d), lambda i: (0, 0)),
                  pl.BlockSpec((1, d), lambda i: (0, 0))],
        out_specs=[pl.BlockSpec((tm, d), lambda i: (i, 0))] * 2,
        out_shape=[jax.ShapeDtypeStruct((n_tok, d), F32), jax.ShapeDtypeStruct((n_tok, d), BF16)],
        compiler_params=_params(("parallel",), 40),
        name="residual_ln",
    )(x, pt, g, b)


def _block_diag(w, reps):
    return jnp.kron(jnp.eye(reps, dtype=w.dtype), w)


def kernel(x_prompt, x_sample, state_hgrn, w_in, sgu_ln_g, sgu_ln_b, sgu_w, sgu_b, hg_lb_logits,
           hg_norm_g, w_branch_a, w_branch_b, w_out, ln1_g, ln1_b, peer_w_q, peer_sub_keys,
           peer_u, peer_v, ln2_g, ln2_b):
    depth = w_in.shape[0]
    n_b, seq, d = x_prompt.shape
    n_s, dec_seq, _ = x_sample.shape
    width = sgu_ln_g.shape[1]
    n_prompt = n_b * seq
    alpha = (2 * depth) ** 0.25
    assert seq % ROWS == 0 and ROWS % dec_seq == 0 and n_s % (ROWS // dec_seq) == 0
    assert w_in.shape[2] == 6 * width + 2 * d and d == 2 * width

    x = jnp.concatenate([x_prompt.reshape(n_prompt, d), x_sample.reshape(n_s * dec_seq, d)], axis=0)
    xb = x.astype(BF16)
    row = lambda a: a.reshape(1, -1)

    tri = jnp.tril(jnp.ones((ROWS, ROWS), F32))
    tri_s = _block_diag(jnp.tril(jnp.ones((dec_seq, dec_seq), F32)), ROWS // dec_seq)

    u_tab, vt_tab = _table_prep(peer_u, peer_v, 512)

    new_sp, new_ss, new_v = [], [], []
    for l in range(depth):
        z = _matmul(xb, w_in, l, 1024, 1024, "in_proj")

        ws = jnp.where(tri > 0, sgu_w[l], 0.0)
        ws_s = jax.vmap(lambda w: _block_diag(w, ROWS // dec_seq))(ws[:, :dec_seq, :dec_seq])
        bias = jnp.repeat(sgu_b[l].T, width // SGU_GROUPS, axis=1)
        bias_s = jnp.tile(bias[:dec_seq], (ROWS // dec_seq, 1))
        common = (row(sgu_ln_g[l]), row(sgu_ln_b[l]))
        tail = (hg_lb_logits, row(hg_norm_g[l]))
        ya_p, yb_p, st_p = _mixer_prompt(l, z, (n_b, seq), width, *common, ws.astype(BF16), bias,
                                         tri.astype(BF16), *tail)
        ya_s, yb_s, vn_s, st_s = _mixer_sample(l, z, n_prompt, (n_s, dec_seq), width, *common,
                                               ws_s.astype(BF16), bias_s, tri_s.astype(BF16), *tail,
                                               state_hgrn)
        ya = jnp.concatenate([ya_p, ya_s], axis=0)
        yb = jnp.concatenate([yb_p, yb_s], axis=0)

        x1, x1t = _merge(alpha, ya, yb, z, x, w_branch_a[l].astype(BF16), w_branch_b[l].astype(BF16),
                         w_out[l].astype(BF16), row(ln1_g[l]), row(ln1_b[l]), 256,
                         (6 * width // d, 6 * width // d + 1))

        thr, c, s2, e2 = _peer_route(x1t, peer_w_q[l].T.astype(BF16), peer_sub_keys[l].astype(BF16), 256)
        pt = _peer_dense(l, x1t, u_tab, vt_tab, thr, c, s2, e2, 512, 1024)
        ln2 = (row(ln2_g[l]), row(ln2_b[l]), 512)
        if l + 1 < depth:
            x, xb = _residual_ln(alpha, x1, pt, *ln2)
        else:
            y_p, _ = _residual_ln(alpha, x1, pt, *ln2, 0, n_prompt)
            y_s, _ = _residual_ln(alpha, x1, pt, *ln2, n_prompt, n_s * dec_seq)

        new_sp.append(st_p)
        new_ss.append(st_s)
        new_v.append(vn_s.reshape(n_s, dec_seq, width))

    return (y_p.reshape(n_b, seq, d), y_s.reshape(n_s, dec_seq, d),
            jnp.stack(new_sp), jnp.stack(new_ss), jnp.stack(new_v))
```

```python
import functools

import jax
import jax.numpy as jnp
from jax import lax
from jax.experimental import pallas as pl
from jax.experimental.pallas import tpu as pltpu

F32 = jnp.float32
BF16 = jnp.bfloat16

SGU_GROUPS = 4
HG_HEADS = 8
PEER_HEADS = 8
PEER_TOPK = 16
LN_EPS = 1e-5
RMS_EPS = 1e-6
ROWS = 128
HG_SUB = 16
MIB = 1024 * 1024

_NT = (((1,), (1,)), ((), ()))
_TN = (((0,), (0,)), ((), ()))


def _params(sem, vmem_mib):
    return pltpu.CompilerParams(dimension_semantics=sem, vmem_limit_bytes=vmem_mib * MIB)


def _gelu(x):
    c = 0.7978845608028654
    half = 0.5 * x
    return half + half * jnp.tanh(x * (c + (c * 0.044715) * (x * x)))


def _sigmoid(x):
    return 1.0 / (1.0 + jnp.exp(-x))


def _layer_norm(x, g, b):
    mu = jnp.mean(x, axis=-1, keepdims=True)
    xc = x - mu
    var = jnp.mean(xc * xc, axis=-1, keepdims=True)
    return xc * lax.rsqrt(var + LN_EPS) * g + b


def _bdot(a, b, dims=None):
    a = a.astype(BF16)
    b = b.astype(BF16)
    if dims is None:
        return jnp.dot(a, b, preferred_element_type=F32)
    return lax.dot_general(a, b, dims, preferred_element_type=F32)


def _two_group_specs(block, head_rows, tile, grid_axis):
    n_head = head_rows // tile
    pick = lambda ids: ids[grid_axis]
    return (n_head,
            pl.BlockSpec(block, lambda *ids: (jnp.minimum(pick(ids), n_head - 1), 0)),
            pl.BlockSpec(block, lambda *ids: (jnp.maximum(pick(ids) - n_head, 0), 0)))


def _matmul_kernel(n_head, xh_ref, xt_ref, w_ref, o_ref, wb_ref):
    i = pl.program_id(1)

    @pl.when(i == 0)
    def _():
        wb_ref[...] = w_ref[...].astype(BF16)

    @pl.when(i < n_head)
    def _():
        o_ref[...] = jnp.dot(xh_ref[...], wb_ref[...], preferred_element_type=F32)

    @pl.when(i >= n_head)
    def _():
        o_ref[...] = jnp.dot(xt_ref[...], wb_ref[...], preferred_element_type=F32)


def _matmul(xs, w, layer, tm, tn, name):
    k = xs[0].shape[1]
    m = xs[0].shape[0] + xs[1].shape[0]
    n = w.shape[2]
    n_head, head_spec, tail_spec = _two_group_specs((tm, k), xs[0].shape[0], tm, 1)
    return pl.pallas_call(
        functools.partial(_matmul_kernel, n_head),
        grid=(n // tn, m // tm),
        in_specs=[head_spec, tail_spec,
                  pl.BlockSpec((None, k, tn), lambda j, i: (layer, 0, j))],
        out_specs=pl.BlockSpec((tm, tn), lambda j, i: (i, j)),
        out_shape=jax.ShapeDtypeStruct((m, n), F32),
        scratch_shapes=[pltpu.VMEM((k, tn), BF16)],
        compiler_params=_params(("parallel", "arbitrary"), 56),
        name=name,
    )(xs[0], xs[1], w)


def _forget_lower_bound(logits_ref, layer):
    lg = logits_ref[...]
    e = jnp.exp(lg - jnp.max(lg, axis=0, keepdims=True))
    p = e / jnp.sum(e, axis=0, keepdims=True)
    return jnp.sum(p[1:layer + 1], axis=0, keepdims=True)


def _hgrn_gates(fp, lb):
    e = jnp.exp(-jnp.abs(fp))
    log_sig = jnp.minimum(fp, 0.0) - jnp.log1p(e)
    sig_neg = jnp.where(fp >= 0.0, e, 1.0) / (1.0 + e)
    if lb is None:
        return log_sig, sig_neg
    k = (1.0 - lb) * sig_neg
    pos = lb > 0.0
    a = jnp.log(jnp.where(pos, lb, 1.0))
    b = jnp.log1p(-lb) + log_sig
    lse = jnp.maximum(a, b) + jnp.log1p(jnp.exp(-jnp.abs(a - b)))
    return jnp.where(pos, lse, b), k


def _segment_cumsum(t_ref, x):
    t = t_ref[...]
    hi = x.astype(BF16)
    r1 = x - hi.astype(F32)
    mid = r1.astype(BF16)
    lo = (r1 - mid.astype(F32)).astype(BF16)
    dot = functools.partial(jnp.dot, preferred_element_type=F32)
    return dot(t, hi) + dot(t, mid) + dot(t, lo)


def _sgu(u, v, lng_ref, lnb_ref, ws_ref, bias_ref, ya_ref):
    vn = _layer_norm(_gelu(v), lng_ref[...], lnb_ref[...])
    vnb = vn.astype(BF16)
    gw = vn.shape[1] // SGU_GROUPS
    for g in range(SGU_GROUPS):
        cs = slice(g * gw, (g + 1) * gw)
        mixed = jnp.dot(ws_ref[g], vnb[:, cs], preferred_element_type=F32) + bias_ref[:, cs]
        ya_ref[:, cs] = (_gelu(u[:, cs]) * mixed).astype(BF16)
    return vn


def _hgrn_out(o, gn, og):
    o = o * lax.rsqrt(jnp.mean(o * o, axis=-1, keepdims=True) + RMS_EPS)
    return (o * gn * (og * _sigmoid(og))).astype(BF16)


def _rows_at(b_ref, size, offset, width):
    parts = []
    for j in range(ROWS // size):
        r = j * size + offset
        if r < 0:
            parts.append(jnp.zeros((size, width), F32))
        else:
            parts.append(jnp.broadcast_to(b_ref[pl.ds(r, 1), :], (size, width)))
    return parts[0] if len(parts) == 1 else jnp.concatenate(parts, axis=0)


def _mixer_prompt_kernel(layer, u_ref, v_ref, q_ref, f_ref, i_ref, og_ref, lng_ref, lnb_ref,
                         ws_ref, bias_ref, tri_ref, lbl_ref, gn_ref,
                         ya_ref, yb_ref, st_ref, stt_ref, b_ref):
    c = pl.program_id(1)
    width = q_ref.shape[1]
    dk = width // HG_HEADS

    @pl.when(c == 0)
    def _():
        stt_ref[...] = jnp.zeros_like(stt_ref)

    _sgu(u_ref[...], v_ref[...], lng_ref, lnb_ref, ws_ref, bias_ref, ya_ref)

    lb = None if layer == 0 else _forget_lower_bound(lbl_ref, layer)
    lf, k = _hgrn_gates(f_ref[...], lb)
    qs = q_ref[...] * (dk ** -0.5)
    b = _segment_cumsum(tri_ref, lf)
    b_ref[...] = b

    e_sub = b - _rows_at(b_ref, HG_SUB, -1, width)
    q_lvl = [qs * jnp.exp(e_sub)]
    k_lvl = [k * jnp.exp(-e_sub)]
    size = HG_SUB
    sizes = []
    while size < ROWS:
        sizes.append(size)
        q_lvl.append(q_lvl[0] if size == HG_SUB else qs * jnp.exp(b - _rows_at(b_ref, size, -1, width)))
        k_lvl.append(k * jnp.exp(_rows_at(b_ref, size, size - 1, width) - b))
        size *= 2
    qd = qs * jnp.exp(b)
    b_end = b_ref[pl.ds(ROWS - 1, 1), :]
    ke = k * jnp.exp(b_end - b)
    g_end = jnp.exp(b_end)

    row = lax.broadcasted_iota(jnp.int32, (ROWS, ROWS), 0)
    col = lax.broadcasted_iota(jnp.int32, (ROWS, ROWS), 1)
    masks = [(row // HG_SUB == col // HG_SUB) & (col <= row)]
    for sz in sizes:
        masks.append(((row // sz) % 2 == 1) & (col // sz == row // sz - 1))

    vv = i_ref[...]
    gn = gn_ref[...]
    og = og_ref[...]
    for h in range(HG_HEADS):
        hs = slice(h * dk, (h + 1) * dk)
        attn = jnp.zeros((ROWS, ROWS), F32)
        for ql, kl, mk in zip(q_lvl, k_lvl, masks):
            attn = jnp.where(mk, _bdot(ql[:, hs], kl[:, hs], _NT), attn)
        vh = vv[:, hs]
        stt = stt_ref[h]
        o = _bdot(attn, vh) + _bdot(qd[:, hs], stt, _NT)
        stt_ref[h] = stt * g_end[:, hs] + _bdot(vh, ke[:, hs], _TN)
        yb_ref[:, hs] = _hgrn_out(o, gn[:, hs], og[:, hs])

    @pl.when(c == pl.num_programs(1) - 1)
    def _():
        for h in range(HG_HEADS):
            st_ref[0, h] = stt_ref[h].T


def _mixer_sample_kernel(layer, seq_len, u_ref, v_ref, q_ref, f_ref, i_ref, og_ref, lng_ref, lnb_ref,
                         ws_ref, bias_ref, tri_ref, lbl_ref, gn_ref, s0_ref,
                         ya_ref, yb_ref, vn_ref, st_ref, b_ref, oi_ref):
    width = q_ref.shape[1]
    dk = width // HG_HEADS
    n_seq = ROWS // seq_len

    vn_ref[...] = _sgu(u_ref[...], v_ref[...], lng_ref, lnb_ref, ws_ref, bias_ref, ya_ref)

    lb = None if layer == 0 else _forget_lower_bound(lbl_ref, layer)
    lf, k = _hgrn_gates(f_ref[...], lb)
    qs = q_ref[...] * (dk ** -0.5)
    b = _segment_cumsum(tri_ref, lf)
    b_ref[...] = b
    qd = qs * jnp.exp(b)
    k0 = k * jnp.exp(-b)
    ke = k * jnp.exp(_rows_at(b_ref, seq_len, seq_len - 1, width) - b)

    row = lax.broadcasted_iota(jnp.int32, (ROWS, ROWS), 0)
    col = lax.broadcasted_iota(jnp.int32, (ROWS, ROWS), 1)
    causal = (row // seq_len == col // seq_len) & (col <= row)
    seq_of_row = lax.broadcasted_iota(jnp.int32, (ROWS, dk), 0) // seq_len

    vv = i_ref[...]
    gn = gn_ref[...]
    og = og_ref[...]
    for h in range(HG_HEADS):
        hs = slice(h * dk, (h + 1) * dk)
        attn = jnp.where(causal, _bdot(qd[:, hs], k0[:, hs], _NT), 0.0)
        vh = vv[:, hs]
        qdh = qd[:, hs].astype(BF16)
        keh = ke[:, hs].astype(BF16)
        oi_ref[...] = _bdot(attn, vh)

        def per_seq(s, carry, h=h, hs=hs, vh=vh, qdh=qdh, keh=keh):
            mine = seq_of_row == s
            s0 = s0_ref[s, h]
            oi_ref[...] += jnp.where(mine, _bdot(qdh, s0), 0.0)
            b_seq = b_ref[pl.ds(pl.multiple_of(s * seq_len, seq_len), seq_len), hs]
            g_end = jnp.exp(b_seq[seq_len - 1:seq_len])
            new_t = s0.T * g_end + _bdot(jnp.where(mine, vh, 0.0), keh, _TN)
            st_ref[s, h] = new_t.T
            return carry

        lax.fori_loop(0, n_seq, per_seq, 0, unroll=4)
        yb_ref[:, hs] = _hgrn_out(oi_ref[...], gn[:, hs], og[:, hs])


def _mixer_specs(width, row_block, n_axes):
    def idx(f):
        return f if n_axes == 2 else (lambda t: f(t, 0))

    def zcol(j):
        return pl.BlockSpec((ROWS, width), idx(lambda a, c, j=j: (row_block(a, c), j)))

    def const(shape):
        return pl.BlockSpec(shape, idx(lambda a, c: (0,) * len(shape)))

    return zcol, const


def _mixer_prompt(layer, z, batch, width, lng, lnb, ws, bias, tri, lbl, gn):
    n_chunks = batch[1] // ROWS
    zcol, const = _mixer_specs(width, lambda a, c: a * n_chunks + c, 2)
    depth = lbl.shape[0]
    dk = width // HG_HEADS
    n_tok = batch[0] * batch[1]
    return pl.pallas_call(
        functools.partial(_mixer_prompt_kernel, layer),
        grid=(batch[0], n_chunks),
        in_specs=[zcol(j) for j in range(6)] + [
            const((1, width)), const((1, width)), const((SGU_GROUPS, ROWS, ROWS)),
            const((ROWS, width)), const((ROWS, ROWS)), const((depth, width)), const((1, width))],
        out_specs=[pl.BlockSpec((ROWS, width), lambda a, c: (a * n_chunks + c, 0)),
                   pl.BlockSpec((ROWS, width), lambda a, c: (a * n_chunks + c, 0)),
                   pl.BlockSpec((1, HG_HEADS, dk, dk), lambda a, c: (a, 0, 0, 0))],
        out_shape=[jax.ShapeDtypeStruct((n_tok, width), BF16),
                   jax.ShapeDtypeStruct((n_tok, width), BF16),
                   jax.ShapeDtypeStruct((batch[0], HG_HEADS, dk, dk), F32)],
        scratch_shapes=[pltpu.VMEM((HG_HEADS, dk, dk), F32), pltpu.VMEM((ROWS, width), F32)],
        compiler_params=_params(("parallel", "arbitrary"), 56),
        name="mixer_prompt",
    )(z, z, z, z, z, z, lng, lnb, ws, bias, tri, lbl, gn)


def _mixer_sample(layer, z, row0, batch, width, lng, lnb, ws, bias, tri, lbl, gn, s0):
    n_seq, seq_len = batch
    per_step = ROWS // seq_len
    steps = n_seq // per_step
    blk0 = row0 // ROWS
    zcol, const = _mixer_specs(width, lambda t, c: blk0 + t, 1)
    depth = lbl.shape[0]
    dk = width // HG_HEADS
    n_tok = n_seq * seq_len
    st_spec = pl.BlockSpec((per_step, HG_HEADS, dk, dk), lambda t: (t, 0, 0, 0))
    row_spec = pl.BlockSpec((ROWS, width), lambda t: (t, 0))
    return pl.pallas_call(
        functools.partial(_mixer_sample_kernel, layer, seq_len),
        grid=(steps,),
        in_specs=[zcol(j) for j in range(6)] + [
            const((1, width)), const((1, width)), const((SGU_GROUPS, ROWS, ROWS)),
            const((ROWS, width)), const((ROWS, ROWS)), const((depth, width)), const((1, width)),
            pl.BlockSpec((None, per_step, HG_HEADS, dk, dk), lambda t: (layer, t, 0, 0, 0))],
        out_specs=[row_spec, row_spec, row_spec, st_spec],
        out_shape=[jax.ShapeDtypeStruct((n_tok, width), BF16),
                   jax.ShapeDtypeStruct((n_tok, width), BF16),
                   jax.ShapeDtypeStruct((n_tok, width), F32),
                   jax.ShapeDtypeStruct((n_seq, HG_HEADS, dk, dk), F32)],
        scratch_shapes=[pltpu.VMEM((ROWS, width), F32), pltpu.VMEM((ROWS, dk), F32)],
        compiler_params=_params(("parallel",), 58),
        name="mixer_sample",
    )(z, z, z, z, z, z, lng, lnb, ws, bias, tri, lbl, gn, s0)


def _merge_kernel(alpha, n_head, ya_ref, yb_ref, ga_ref, gb_ref, xh_ref, xt_ref, wpa_ref, wpb_ref, wo_ref,
                  g_ref, b_ref, x1_ref, x1t_ref):
    a = jnp.dot(ya_ref[...], wpa_ref[...], preferred_element_type=F32)
    b = jnp.dot(yb_ref[...], wpb_ref[...], preferred_element_type=F32)
    merged = _sigmoid(ga_ref[...]) * a + _sigmoid(gb_ref[...]) * b
    y = jnp.dot(merged.astype(BF16), wo_ref[...], preferred_element_type=F32)
    x = jnp.where(pl.program_id(0) < n_head, xh_ref[...], xt_ref[...])
    x1 = _layer_norm(alpha * x + y, g_ref[...], b_ref[...])
    x1_ref[...] = x1
    x1t_ref[...] = x1.T.astype(BF16)


def _merge(alpha, ya, yb, z, xs, wpa, wpb, wo, g, b, tm, gate_blocks):
    d = xs[0].shape[1]
    n_tok = xs[0].shape[0] + xs[1].shape[0]
    da = ya.shape[1]
    ga_blk, gb_blk = gate_blocks
    const = lambda shape: pl.BlockSpec(shape, lambda i: (0,) * len(shape), pipeline_mode=pl.Buffered(1))
    n_head, head_spec, tail_spec = _two_group_specs((tm, d), xs[0].shape[0], tm, 0)
    return pl.pallas_call(
        functools.partial(_merge_kernel, alpha, n_head),
        grid=(n_tok // tm,),
        in_specs=[pl.BlockSpec((tm, da), lambda i: (i, 0)),
                  pl.BlockSpec((tm, da), lambda i: (i, 0)),
                  pl.BlockSpec((tm, d), lambda i: (i, ga_blk)),
                  pl.BlockSpec((tm, d), lambda i: (i, gb_blk)),
                  head_spec, tail_spec,
                  const((da, d)), const((da, d)), const((d, d)), const((1, d)), const((1, d))],
        out_specs=[pl.BlockSpec((tm, d), lambda i: (i, 0)),
                   pl.BlockSpec((d, tm), lambda i: (0, i))],
        out_shape=[jax.ShapeDtypeStruct((n_tok, d), F32),
                   jax.ShapeDtypeStruct((d, n_tok), BF16)],
        compiler_params=_params(("parallel",), 58),
        name="merge_ln",
    )(ya, yb, z, z, xs[0], xs[1], wpa, wpb, wo, g, b)


def _compare_exchange(v, i, l):
    v[i], v[l] = jnp.maximum(v[i], v[l]), jnp.minimum(v[i], v[l])


def _bitonic_merge(v):
    n = len(v)
    j = n // 2
    while j:
        for i in range(n):
            if i ^ j > i:
                _compare_exchange(v, i, i ^ j)
        j //= 2
    return v


def _top_sorted(a):
    sub = 8
    n = a.shape[0] // sub
    v = [a[sub * k:sub * (k + 1)] for k in range(n)]
    k = 2
    while k <= n:
        j = k // 2
        while j:
            for i in range(n):
                l = i ^ j
                if l > i:
                    if i & k == 0:
                        _compare_exchange(v, i, l)
                    else:
                        _compare_exchange(v, l, i)
            j //= 2
        k *= 2
    shift = sub // 2
    while shift:
        other = [pltpu.roll(t, shift, 0) for t in v]
        v = _bitonic_merge([jnp.maximum(v[i], other[n - 1 - i]) for i in range(n)])
        shift //= 2
    return [t[0:1] for t in v]


def _stack_rows(rows, n):
    lanes = rows[0].shape[1]
    idx = lax.broadcasted_iota(jnp.int32, (n, lanes), 0)
    acc = jnp.broadcast_to(rows[0], (n, lanes))
    for r in range(1, n):
        acc = jnp.where(idx == r, rows[r], acc)
    return acc


def _peer_route_kernel(xt_ref, wqt_ref, sk_ref, thr_ref, c_ref, s2_ref, e2_ref, q_ref):
    q_ref[...] = jnp.dot(wqt_ref[...], xt_ref[...], preferred_element_type=F32).astype(BF16)
    half = sk_ref.shape[3]

    def per_head(h, carry):
        base = pl.multiple_of(h * 2 * half, 2 * half)
        s1 = jnp.dot(sk_ref[h, 0], q_ref[pl.ds(base, half), :], preferred_element_type=F32)
        s2 = jnp.dot(sk_ref[h, 1], q_ref[pl.ds(base + half, half), :], preferred_element_type=F32)
        t1 = _top_sorted(s1)
        t2 = _top_sorted(s2)
        t2_all = _stack_rows(t2, PEER_TOPK)
        t2_half = t2_all[:PEER_TOPK // 2]
        sums = [t1[a] + t2[b] for a in range(3, PEER_TOPK) for b in range(PEER_TOPK // (a + 1))]
        sums += [jnp.full_like(t1[0], -jnp.inf)] * (-len(sums) % 8)
        packed = [_stack_rows(sums[r:r + 8], 8) for r in range(0, len(sums), 8)]
        cand = jnp.concatenate([t1[0] + t2_all, t1[1] + t2_half, t1[2] + t2_half] + packed, axis=0)
        above = jnp.zeros_like(t1[0])
        tau = t1[0] + t2[0]
        for _ in range(PEER_TOPK):
            m = jnp.max(cand, axis=0, keepdims=True)
            eq = cand == m
            tau = jnp.where(above < PEER_TOPK, m, tau)
            above = above + jnp.sum(jnp.where(eq, 1.0, 0.0), axis=0, keepdims=True)
            cand = jnp.where(eq, -jnp.inf, cand)
        rows = [t1[a] + t2_all for a in range(PEER_TOPK)]
        count = lambda mask: jnp.sum(jnp.where(mask, 1.0, 0.0), axis=0, keepdims=True)
        n_gt = [count(r > tau) for r in rows]
        n_eq = [count(r == tau) for r in rows]
        spare = float(PEER_TOPK) - sum(n_gt)
        rank = lax.broadcasted_iota(jnp.int32, t2_all.shape, 0).astype(F32)
        top_score = t1[0] + t2[0]
        z = jnp.zeros_like(tau)
        thr = jnp.full(s1.shape, jnp.inf, F32)
        for a in range(PEER_TOPK):
            take = jnp.minimum(n_eq[a], spare)
            spare = spare - take
            keep = rank < n_gt[a] + take
            z = z + jnp.sum(jnp.where(keep, jnp.exp(rows[a] - top_score), 0.0), axis=0, keepdims=True)
            lowest_kept = jnp.min(jnp.where(keep, t2_all, jnp.inf), axis=0, keepdims=True)
            thr = jnp.where(s1 == t1[a], lowest_kept, thr)
        thr_ref[h] = thr
        c_ref[h] = jnp.exp(s1 - t1[0]) / z
        s2_ref[h] = s2
        e2_ref[h] = jnp.exp(s2 - t2[0])
        return carry

    lax.fori_loop(0, PEER_HEADS, per_head, 0)


def _peer_route(xt, wqt, sk, tb):
    d, n_tok = xt.shape
    nk = sk.shape[2]
    out = jax.ShapeDtypeStruct((PEER_HEADS, nk, n_tok), F32)
    ospec = pl.BlockSpec((PEER_HEADS, nk, tb), lambda i: (0, 0, i))
    return pl.pallas_call(
        _peer_route_kernel,
        grid=(n_tok // tb,),
        in_specs=[pl.BlockSpec((d, tb), lambda i: (0, i)),
                  pl.BlockSpec(wqt.shape, lambda i: (0, 0)),
                  pl.BlockSpec(sk.shape, lambda i: (0, 0, 0, 0))],
        out_specs=[ospec] * 4,
        out_shape=[out] * 4,
        scratch_shapes=[pltpu.VMEM((wqt.shape[0], tb), BF16)],
        compiler_params=_params(("parallel",), 48),
        name="peer_route",
    )(xt, wqt, sk)


def _peer_dense_kernel(n_tiles, n_exp_tiles, lane_chunk, xt_ref, u_ref, vt_ref, thr_ref, c_ref,
                       s2_ref, e2_ref, out_ref, ht_a, ht_b, pt_a, pt_b):
    s = pl.program_id(0)
    retrieved = jnp.clip(s - 2, 0, n_tiles - 1)

    @pl.when(s == 0)
    def _():
        for ref in (ht_a, ht_b, pt_a, pt_b):
            ref[...] = jnp.zeros_like(ref)

    @pl.when(retrieved % n_exp_tiles == 0)
    def _():
        out_ref[...] = jnp.zeros_like(out_ref)

    nk = s2_ref.shape[1]
    tb = xt_ref.shape[1]
    n_keys = u_ref.shape[0] // nk
    sub = 16

    def stages(ht_new, ht_old, pt_new, pt_old):
        def project(rs, ls):
            ht_new[rs, ls] = jnp.dot(u_ref[rs, :], xt_ref[:, ls], preferred_element_type=F32)

        def gate(keys, j0, ls):
            acc = {}
            for h in range(PEER_HEADS):
                s2 = s2_ref[h, j0:j0 + sub, ls]
                e2 = e2_ref[h, j0:j0 + sub, ls]
                for ii in keys:
                    hit = s2 >= thr_ref[h, ii:ii + 1, ls]
                    term = jnp.where(hit, e2, 0.0) * c_ref[h, ii:ii + 1, ls]
                    acc[ii] = term if h == 0 else acc[ii] + term
            for ii in keys:
                rs = slice(ii * nk + j0, ii * nk + j0 + sub)
                pt_new[rs, ls] = (acc[ii] * _gelu(ht_old[rs, ls])).astype(BF16)

        def retrieve(rs, ls):
            out_ref[rs, ls] += jnp.dot(vt_ref[rs, :], pt_old[:, ls], preferred_element_type=F32)

        lanes = [slice(t0, t0 + lane_chunk) for t0 in range(0, tb, lane_chunk)]
        blk = 256
        p_units = [functools.partial(project, slice(r0, r0 + blk), ls)
                   for ls in lanes for r0 in range(0, u_ref.shape[0], blk)]
        r_units = [functools.partial(retrieve, slice(r0, r0 + blk), ls)
                   for ls in lanes for r0 in range(0, vt_ref.shape[0], blk)]
        key_groups = [range(n_keys)]
        g_units = [functools.partial(gate, keys, j0, slice(g0, g0 + 128))
                   for g0 in range(0, tb, 128) for j0 in range(0, nk, sub) for keys in key_groups]
        for n, g_unit in enumerate(g_units):
            for units in (p_units, r_units):
                if n * len(units) % len(g_units) == 0:
                    units[n * len(units) // len(g_units)]()
            g_unit()

    @pl.when(s % 2 == 0)
    def _():
        stages(ht_a, ht_b, pt_b, pt_a)

    @pl.when(s % 2 == 1)
    def _():
        stages(ht_b, ht_a, pt_a, pt_b)


def _table_prep_kernel(u_ref, v_ref, ub_ref, vt_ref):
    ub_ref[...] = u_ref[...].astype(BF16)
    vt_ref[...] = v_ref[...].T.astype(BF16)


def _table_prep(peer_u, peer_v, eb):
    depth, n_exp, d = peer_u.shape
    in_spec = pl.BlockSpec((None, eb, d), lambda l, e: (l, e, 0))
    return pl.pallas_call(
        _table_prep_kernel,
        grid=(depth, n_exp // eb),
        in_specs=[in_spec, in_spec],
        out_specs=[pl.BlockSpec((None, eb, d), lambda l, e: (l, e, 0)),
                   pl.BlockSpec((None, d, eb), lambda l, e: (l, 0, e))],
        out_shape=[jax.ShapeDtypeStruct((depth, n_exp, d), BF16),
                   jax.ShapeDtypeStruct((depth, d, n_exp), BF16)],
        compiler_params=_params(("parallel", "parallel"), 48),
        name="table_prep",
    )(peer_u, peer_v)


def _peer_dense(layer, xt, u, vt, thr, c, s2, e2, tb, eb):
    d, n_tok = xt.shape
    n_exp = u.shape[1]
    nk = s2.shape[1]
    keys_per_blk = eb // nk
    n_e = n_exp // eb
    n_tiles = (n_tok // tb) * n_e
    proj = lambda s: jnp.minimum(s, n_tiles - 1)
    gate = lambda s: jnp.clip(s - 1, 0, n_tiles - 1)
    retr = lambda s: jnp.clip(s - 2, 0, n_tiles - 1)
    row_spec = pl.BlockSpec((PEER_HEADS, keys_per_blk, tb), lambda s: (0, gate(s) % n_e, gate(s) // n_e))
    all_spec = pl.BlockSpec((PEER_HEADS, nk, tb), lambda s: (0, 0, gate(s) // n_e))
    return pl.pallas_call(
        functools.partial(_peer_dense_kernel, n_tiles, n_e, 256),
        grid=(n_tiles + 2,),
        in_specs=[pl.BlockSpec((d, tb), lambda s: (0, proj(s) // n_e)),
                  pl.BlockSpec((None, eb, d), lambda s: (layer, proj(s) % n_e, 0)),
                  pl.BlockSpec((None, d, eb), lambda s: (layer, 0, retr(s) % n_e)),
                  row_spec, row_spec, all_spec, all_spec],
        out_specs=pl.BlockSpec((d, tb), lambda s: (0, retr(s) // n_e)),
        out_shape=jax.ShapeDtypeStruct((d, n_tok), F32),
        scratch_shapes=[pltpu.VMEM((eb, tb), F32), pltpu.VMEM((eb, tb), F32),
                        pltpu.VMEM((eb, tb), BF16), pltpu.VMEM((eb, tb), BF16)],
        compiler_params=_params(("arbitrary",), 58),
        name="peer_dense",
    )(xt, u, vt, thr, c, s2, e2)


def _residual_ln_kernel(alpha, x_ref, pt_ref, g_ref, b_ref, o_ref, ob_ref):
    y = _layer_norm(alpha * x_ref[...] + pt_ref[...].T, g_ref[...], b_ref[...])
    o_ref[...] = y
    ob_ref[...] = y.astype(BF16)


def _residual_ln(alpha, x, pt, g, b, tm, row0=0, n_tok=None):
    d = x.shape[1]
    n_tok = x.shape[0] if n_tok is None else n_tok
    blk0 = row0 // tm
    return pl.pallas_call(
        functools.partial(_residual_ln_kernel, alpha),
        grid=(n_tok // tm,),
        in_specs=[pl.BlockSpec((tm, d), lambda i: (blk0 + i, 0)),
                  pl.BlockSpec((d, tm), lambda i: (0, blk0 + i)),
                  pl.BlockSpec((1, d), lambda i: (0, 0)),
                  pl.BlockSpec((1, d), lambda i: (0, 0))],
        out_specs=[pl.BlockSpec((tm, d), lambda i: (i, 0))] * 2,
        out_shape=[jax.ShapeDtypeStruct((n_tok, d), F32), jax.ShapeDtypeStruct((n_tok, d), BF16)],
        compiler_params=_params(("parallel",), 40),
        name="residual_ln",
    )(x, pt, g, b)


def _block_diag(w, reps):
    return jnp.kron(jnp.eye(reps, dtype=w.dtype), w)


def kernel(x_prompt, x_sample, state_hgrn, w_in, sgu_ln_g, sgu_ln_b, sgu_w, sgu_b, hg_lb_logits,
           hg_norm_g, w_branch_a, w_branch_b, w_out, ln1_g, ln1_b, peer_w_q, peer_sub_keys,
           peer_u, peer_v, ln2_g, ln2_b):
    depth = w_in.shape[0]
    n_b, seq, d = x_prompt.shape
    n_s, dec_seq, _ = x_sample.shape
    width = sgu_ln_g.shape[1]
    n_prompt = n_b * seq
    alpha = (2 * depth) ** 0.25
    assert seq % ROWS == 0 and ROWS % dec_seq == 0 and n_s % (ROWS // dec_seq) == 0
    assert w_in.shape[2] == 6 * width + 2 * d and d == 2 * width
    assert n_prompt % 1024 == 0 and (n_s * dec_seq) % 1024 == 0

    xs = (x_prompt.reshape(n_prompt, d), x_sample.reshape(n_s * dec_seq, d))
    xs_b = tuple(a.astype(BF16) for a in xs)
    row = lambda a: a.reshape(1, -1)

    tri = jnp.tril(jnp.ones((ROWS, ROWS), F32))
    tri_s = _block_diag(jnp.tril(jnp.ones((dec_seq, dec_seq), F32)), ROWS // dec_seq)

    u_tab, vt_tab = _table_prep(peer_u, peer_v, 512)

    new_sp, new_ss, new_v = [], [], []
    for l in range(depth):
        z = _matmul(xs_b, w_in, l, 1024, 1024, "in_proj")

        ws = jnp.where(tri > 0, sgu_w[l], 0.0)
        ws_s = jax.vmap(lambda w: _block_diag(w, ROWS // dec_seq))(ws[:, :dec_seq, :dec_seq])
        bias = jnp.repeat(sgu_b[l].T, width // SGU_GROUPS, axis=1)
        bias_s = jnp.tile(bias[:dec_seq], (ROWS // dec_seq, 1))
        common = (row(sgu_ln_g[l]), row(sgu_ln_b[l]))
        tail = (hg_lb_logits, row(hg_norm_g[l]))
        ya_p, yb_p, st_p = _mixer_prompt(l, z, (n_b, seq), width, *common, ws.astype(BF16), bias,
                                         tri.astype(BF16), *tail)
        ya_s, yb_s, vn_s, st_s = _mixer_sample(l, z, n_prompt, (n_s, dec_seq), width, *common,
                                               ws_s.astype(BF16), bias_s, tri_s.astype(BF16), *tail,
                                               state_hgrn)
        ya = jnp.concatenate([ya_p, ya_s], axis=0)
        yb = jnp.concatenate([yb_p, yb_s], axis=0)

        x1, x1t = _merge(alpha, ya, yb, z, xs, w_branch_a[l].astype(BF16), w_branch_b[l].astype(BF16),
                         w_out[l].astype(BF16), row(ln1_g[l]), row(ln1_b[l]), 256,
                         (6 * width // d, 6 * width // d + 1))

        thr, c, s2, e2 = _peer_route(x1t, peer_w_q[l].T.astype(BF16), peer_sub_keys[l].astype(BF16), 256)
        pt = _peer_dense(l, x1t, u_tab, vt_tab, thr, c, s2, e2, 512, 1024)
        ln2 = (row(ln2_g[l]), row(ln2_b[l]), 512)
        y_p, yb_p = _residual_ln(alpha, x1, pt, *ln2, 0, n_prompt)
        y_s, yb_s = _residual_ln(alpha, x1, pt, *ln2, n_prompt, n_s * dec_seq)
        xs, xs_b = (y_p, y_s), (yb_p, yb_s)

        new_sp.append(st_p)
        new_ss.append(st_s)
        new_v.append(vn_s.reshape(n_s, dec_seq, width))

    return (xs[0].reshape(n_b, seq, d), xs[1].reshape(n_s, dec_seq, d),
            jnp.stack(new_sp), jnp.stack(new_ss), jnp.stack(new_v))
```

```python
import functools

import jax
import jax.numpy as jnp
from jax import lax
from jax.experimental import pallas as pl
from jax.experimental.pallas import tpu as pltpu

F32 = jnp.float32
BF16 = jnp.bfloat16

SGU_GROUPS = 4
HG_HEADS = 8
PEER_HEADS = 8
PEER_TOPK = 16
LN_EPS = 1e-5
RMS_EPS = 1e-6
ROWS = 128
HG_SUB = 16
MIB = 1024 * 1024

_NT = (((1,), (1,)), ((), ()))
_TN = (((0,), (0,)), ((), ()))


def _params(sem, vmem_mib):
    return pltpu.CompilerParams(dimension_semantics=sem, vmem_limit_bytes=vmem_mib * MIB)


def _gelu(x):
    c = 0.7978845608028654
    half = 0.5 * x
    return half + half * jnp.tanh(x * (c + (c * 0.044715) * (x * x)))


def _sigmoid(x):
    return 1.0 / (1.0 + jnp.exp(-x))


def _layer_norm(x, g, b):
    mu = jnp.mean(x, axis=-1, keepdims=True)
    xc = x - mu
    var = jnp.mean(xc * xc, axis=-1, keepdims=True)
    return xc * lax.rsqrt(var + LN_EPS) * g + b


def _bdot(a, b, dims=None):
    a = a.astype(BF16)
    b = b.astype(BF16)
    if dims is None:
        return jnp.dot(a, b, preferred_element_type=F32)
    return lax.dot_general(a, b, dims, preferred_element_type=F32)


def _two_group_specs(block, head_rows, tile, grid_axis):
    n_head = head_rows // tile
    pick = lambda ids: ids[grid_axis]
    return (n_head,
            pl.BlockSpec(block, lambda *ids: (jnp.minimum(pick(ids), n_head - 1), 0)),
            pl.BlockSpec(block, lambda *ids: (jnp.maximum(pick(ids) - n_head, 0), 0)))


def _matmul_kernel(n_head, xh_ref, xt_ref, w_ref, o_ref, wb_ref):
    i = pl.program_id(1)

    @pl.when(i == 0)
    def _():
        wb_ref[...] = w_ref[...].astype(BF16)

    @pl.when(i < n_head)
    def _():
        o_ref[...] = jnp.dot(xh_ref[...], wb_ref[...], preferred_element_type=F32)

    @pl.when(i >= n_head)
    def _():
        o_ref[...] = jnp.dot(xt_ref[...], wb_ref[...], preferred_element_type=F32)


def _matmul(xs, w, layer, tm, tn, name):
    k = xs[0].shape[1]
    m = xs[0].shape[0] + xs[1].shape[0]
    n = w.shape[2]
    n_head, head_spec, tail_spec = _two_group_specs((tm, k), xs[0].shape[0], tm, 1)
    return pl.pallas_call(
        functools.partial(_matmul_kernel, n_head),
        grid=(n // tn, m // tm),
        in_specs=[head_spec, tail_spec,
                  pl.BlockSpec((None, k, tn), lambda j, i: (layer, 0, j))],
        out_specs=pl.BlockSpec((tm, tn), lambda j, i: (i, j)),
        out_shape=jax.ShapeDtypeStruct((m, n), F32),
        scratch_shapes=[pltpu.VMEM((k, tn), BF16)],
        compiler_params=_params(("parallel", "arbitrary"), 56),
        name=name,
    )(xs[0], xs[1], w)


def _forget_lower_bound(logits_ref, layer):
    lg = logits_ref[...]
    e = jnp.exp(lg - jnp.max(lg, axis=0, keepdims=True))
    p = e / jnp.sum(e, axis=0, keepdims=True)
    return jnp.sum(p[1:layer + 1], axis=0, keepdims=True)


def _hgrn_gates(fp, lb):
    e = jnp.exp(-jnp.abs(fp))
    log_sig = jnp.minimum(fp, 0.0) - jnp.log1p(e)
    sig_neg = jnp.where(fp >= 0.0, e, 1.0) / (1.0 + e)
    if lb is None:
        return log_sig, sig_neg
    k = (1.0 - lb) * sig_neg
    pos = lb > 0.0
    a = jnp.log(jnp.where(pos, lb, 1.0))
    b = jnp.log1p(-lb) + log_sig
    lse = jnp.maximum(a, b) + jnp.log1p(jnp.exp(-jnp.abs(a - b)))
    return jnp.where(pos, lse, b), k


def _segment_cumsum(t_ref, x):
    t = t_ref[...]
    hi = x.astype(BF16)
    r1 = x - hi.astype(F32)
    mid = r1.astype(BF16)
    lo = (r1 - mid.astype(F32)).astype(BF16)
    dot = functools.partial(jnp.dot, preferred_element_type=F32)
    return dot(t, hi) + dot(t, mid) + dot(t, lo)


def _sgu(u, v, lng_ref, lnb_ref, ws_ref, bias_ref, ya_ref):
    vn = _layer_norm(_gelu(v), lng_ref[...], lnb_ref[...])
    vnb = vn.astype(BF16)
    gw = vn.shape[1] // SGU_GROUPS
    for g in range(SGU_GROUPS):
        cs = slice(g * gw, (g + 1) * gw)
        mixed = jnp.dot(ws_ref[g], vnb[:, cs], preferred_element_type=F32) + bias_ref[:, cs]
        ya_ref[:, cs] = (_gelu(u[:, cs]) * mixed).astype(BF16)
    return vn


def _hgrn_out(o, gn, og):
    o = o * lax.rsqrt(jnp.mean(o * o, axis=-1, keepdims=True) + RMS_EPS)
    return (o * gn * (og * _sigmoid(og))).astype(BF16)


def _rows_at(b_ref, size, offset, width):
    parts = []
    for j in range(ROWS // size):
        r = j * size + offset
        if r < 0:
            parts.append(jnp.zeros((size, width), F32))
        else:
            parts.append(jnp.broadcast_to(b_ref[pl.ds(r, 1), :], (size, width)))
    return parts[0] if len(parts) == 1 else jnp.concatenate(parts, axis=0)


def _mixer_prompt_kernel(layer, u_ref, v_ref, q_ref, f_ref, i_ref, og_ref, lng_ref, lnb_ref,
                         ws_ref, bias_ref, tri_ref, lbl_ref, gn_ref,
                         ya_ref, yb_ref, st_ref, stt_ref, b_ref):
    c = pl.program_id(1)
    width = q_ref.shape[1]
    dk = width // HG_HEADS

    @pl.when(c == 0)
    def _():
        stt_ref[...] = jnp.zeros_like(stt_ref)

    _sgu(u_ref[...], v_ref[...], lng_ref, lnb_ref, ws_ref, bias_ref, ya_ref)

    lb = None if layer == 0 else _forget_lower_bound(lbl_ref, layer)
    lf, k = _hgrn_gates(f_ref[...], lb)
    qs = q_ref[...] * (dk ** -0.5)
    b = _segment_cumsum(tri_ref, lf)
    b_ref[...] = b

    e_sub = b - _rows_at(b_ref, HG_SUB, -1, width)
    q_lvl = [qs * jnp.exp(e_sub)]
    k_lvl = [k * jnp.exp(-e_sub)]
    size = HG_SUB
    sizes = []
    while size < ROWS:
        sizes.append(size)
        q_lvl.append(q_lvl[0] if size == HG_SUB else qs * jnp.exp(b - _rows_at(b_ref, size, -1, width)))
        k_lvl.append(k * jnp.exp(_rows_at(b_ref, size, size - 1, width) - b))
        size *= 2
    qd = qs * jnp.exp(b)
    b_end = b_ref[pl.ds(ROWS - 1, 1), :]
    ke = k * jnp.exp(b_end - b)
    g_end = jnp.exp(b_end)

    row = lax.broadcasted_iota(jnp.int32, (ROWS, ROWS), 0)
    col = lax.broadcasted_iota(jnp.int32, (ROWS, ROWS), 1)
    masks = [(row // HG_SUB == col // HG_SUB) & (col <= row)]
    for sz in sizes:
        masks.append(((row // sz) % 2 == 1) & (col // sz == row // sz - 1))

    vv = i_ref[...]
    gn = gn_ref[...]
    og = og_ref[...]
    for h in range(HG_HEADS):
        hs = slice(h * dk, (h + 1) * dk)
        attn = jnp.zeros((ROWS, ROWS), F32)
        for ql, kl, mk in zip(q_lvl, k_lvl, masks):
            attn = jnp.where(mk, _bdot(ql[:, hs], kl[:, hs], _NT), attn)
        vh = vv[:, hs]
        stt = stt_ref[h]
        o = _bdot(attn, vh) + _bdot(qd[:, hs], stt, _NT)
        stt_ref[h] = stt * g_end[:, hs] + _bdot(vh, ke[:, hs], _TN)
        yb_ref[:, hs] = _hgrn_out(o, gn[:, hs], og[:, hs])

    @pl.when(c == pl.num_programs(1) - 1)
    def _():
        for h in range(HG_HEADS):
            st_ref[0, h] = stt_ref[h].T


def _mixer_sample_kernel(layer, seq_len, u_ref, v_ref, q_ref, f_ref, i_ref, og_ref, lng_ref, lnb_ref,
                         ws_ref, bias_ref, tri_ref, lbl_ref, gn_ref, s0_ref,
                         ya_ref, yb_ref, vn_ref, st_ref, b_ref, oi_ref):
    width = q_ref.shape[1]
    dk = width // HG_HEADS
    n_seq = ROWS // seq_len

    vn_ref[...] = _sgu(u_ref[...], v_ref[...], lng_ref, lnb_ref, ws_ref, bias_ref, ya_ref)

    lb = None if layer == 0 else _forget_lower_bound(lbl_ref, layer)
    lf, k = _hgrn_gates(f_ref[...], lb)
    qs = q_ref[...] * (dk ** -0.5)
    b = _segment_cumsum(tri_ref, lf)
    b_ref[...] = b
    qd = qs * jnp.exp(b)
    k0 = k * jnp.exp(-b)
    ke = k * jnp.exp(_rows_at(b_ref, seq_len, seq_len - 1, width) - b)

    row = lax.broadcasted_iota(jnp.int32, (ROWS, ROWS), 0)
    col = lax.broadcasted_iota(jnp.int32, (ROWS, ROWS), 1)
    causal = (row // seq_len == col // seq_len) & (col <= row)
    seq_of_row = lax.broadcasted_iota(jnp.int32, (ROWS, dk), 0) // seq_len

    vv = i_ref[...]
    gn = gn_ref[...]
    og = og_ref[...]
    for h in range(HG_HEADS):
        hs = slice(h * dk, (h + 1) * dk)
        attn = jnp.where(causal, _bdot(qd[:, hs], k0[:, hs], _NT), 0.0)
        vh = vv[:, hs]
        qdh = qd[:, hs].astype(BF16)
        keh = ke[:, hs].astype(BF16)
        oi_ref[...] = _bdot(attn, vh)

        def per_seq(s, carry, h=h, hs=hs, vh=vh, qdh=qdh, keh=keh):
            mine = seq_of_row == s
            s0 = s0_ref[s, h]
            oi_ref[...] += jnp.where(mine, _bdot(qdh, s0), 0.0)
            b_seq = b_ref[pl.ds(pl.multiple_of(s * seq_len, seq_len), seq_len), hs]
            g_end = jnp.exp(b_seq[seq_len - 1:seq_len])
            new_t = s0.T * g_end + _bdot(jnp.where(mine, vh, 0.0), keh, _TN)
            st_ref[s, h] = new_t.T
            return carry

        lax.fori_loop(0, n_seq, per_seq, 0, unroll=4)
        yb_ref[:, hs] = _hgrn_out(oi_ref[...], gn[:, hs], og[:, hs])


def _mixer_specs(width, row_block, n_axes):
    def idx(f):
        return f if n_axes == 2 else (lambda t: f(t, 0))

    def zcol(j):
        return pl.BlockSpec((ROWS, width), idx(lambda a, c, j=j: (row_block(a, c), j)))

    def const(shape):
        return pl.BlockSpec(shape, idx(lambda a, c: (0,) * len(shape)))

    return zcol, const


def _mixer_prompt(layer, z, batch, width, lng, lnb, ws, bias, tri, lbl, gn):
    n_chunks = batch[1] // ROWS
    zcol, const = _mixer_specs(width, lambda a, c: a * n_chunks + c, 2)
    depth = lbl.shape[0]
    dk = width // HG_HEADS
    n_tok = batch[0] * batch[1]
    return pl.pallas_call(
        functools.partial(_mixer_prompt_kernel, layer),
        grid=(batch[0], n_chunks),
        in_specs=[zcol(j) for j in range(6)] + [
            const((1, width)), const((1, width)), const((SGU_GROUPS, ROWS, ROWS)),
            const((ROWS, width)), const((ROWS, ROWS)), const((depth, width)), const((1, width))],
        out_specs=[pl.BlockSpec((ROWS, width), lambda a, c: (a * n_chunks + c, 0)),
                   pl.BlockSpec((ROWS, width), lambda a, c: (a * n_chunks + c, 0)),
                   pl.BlockSpec((1, HG_HEADS, dk, dk), lambda a, c: (a, 0, 0, 0))],
        out_shape=[jax.ShapeDtypeStruct((n_tok, width), BF16),
                   jax.ShapeDtypeStruct((n_tok, width), BF16),
                   jax.ShapeDtypeStruct((batch[0], HG_HEADS, dk, dk), F32)],
        scratch_shapes=[pltpu.VMEM((HG_HEADS, dk, dk), F32), pltpu.VMEM((ROWS, width), F32)],
        compiler_params=_params(("parallel", "arbitrary"), 56),
        name="mixer_prompt",
    )(z, z, z, z, z, z, lng, lnb, ws, bias, tri, lbl, gn)


def _mixer_sample(layer, z, row0, batch, width, lng, lnb, ws, bias, tri, lbl, gn, s0):
    n_seq, seq_len = batch
    per_step = ROWS // seq_len
    steps = n_seq // per_step
    blk0 = row0 // ROWS
    zcol, const = _mixer_specs(width, lambda t, c: blk0 + t, 1)
    depth = lbl.shape[0]
    dk = width // HG_HEADS
    n_tok = n_seq * seq_len
    st_spec = pl.BlockSpec((per_step, HG_HEADS, dk, dk), lambda t: (t, 0, 0, 0))
    row_spec = pl.BlockSpec((ROWS, width), lambda t: (t, 0))
    return pl.pallas_call(
        functools.partial(_mixer_sample_kernel, layer, seq_len),
        grid=(steps,),
        in_specs=[zcol(j) for j in range(6)] + [
            const((1, width)), const((1, width)), const((SGU_GROUPS, ROWS, ROWS)),
            const((ROWS, width)), const((ROWS, ROWS)), const((depth, width)), const((1, width)),
            pl.BlockSpec((None, per_step, HG_HEADS, dk, dk), lambda t: (layer, t, 0, 0, 0))],
        out_specs=[row_spec, row_spec, row_spec, st_spec],
        out_shape=[jax.ShapeDtypeStruct((n_tok, width), BF16),
                   jax.ShapeDtypeStruct((n_tok, width), BF16),
                   jax.ShapeDtypeStruct((n_tok, width), F32),
                   jax.ShapeDtypeStruct((n_seq, HG_HEADS, dk, dk), F32)],
        scratch_shapes=[pltpu.VMEM((ROWS, width), F32), pltpu.VMEM((ROWS, dk), F32)],
        compiler_params=_params(("parallel",), 58),
        name="mixer_sample",
    )(z, z, z, z, z, z, lng, lnb, ws, bias, tri, lbl, gn, s0)


def _merge_kernel(alpha, n_head, ya_ref, yb_ref, ga_ref, gb_ref, xh_ref, xt_ref, wpa_ref, wpb_ref, wo_ref,
                  g_ref, b_ref, x1_ref, x1t_ref):
    a = jnp.dot(ya_ref[...], wpa_ref[...], preferred_element_type=F32)
    b = jnp.dot(yb_ref[...], wpb_ref[...], preferred_element_type=F32)
    merged = _sigmoid(ga_ref[...]) * a + _sigmoid(gb_ref[...]) * b
    y = jnp.dot(merged.astype(BF16), wo_ref[...], preferred_element_type=F32)
    x = jnp.where(pl.program_id(0) < n_head, xh_ref[...], xt_ref[...])
    x1 = _layer_norm(alpha * x + y, g_ref[...], b_ref[...])
    x1_ref[...] = x1
    x1t_ref[...] = x1.T.astype(BF16)


def _merge(alpha, ya, yb, z, xs, wpa, wpb, wo, g, b, tm, gate_blocks):
    d = xs[0].shape[1]
    n_tok = xs[0].shape[0] + xs[1].shape[0]
    da = ya.shape[1]
    ga_blk, gb_blk = gate_blocks
    const = lambda shape: pl.BlockSpec(shape, lambda i: (0,) * len(shape), pipeline_mode=pl.Buffered(1))
    n_head, head_spec, tail_spec = _two_group_specs((tm, d), xs[0].shape[0], tm, 0)
    return pl.pallas_call(
        functools.partial(_merge_kernel, alpha, n_head),
        grid=(n_tok // tm,),
        in_specs=[pl.BlockSpec((tm, da), lambda i: (i, 0)),
                  pl.BlockSpec((tm, da), lambda i: (i, 0)),
                  pl.BlockSpec((tm, d), lambda i: (i, ga_blk)),
                  pl.BlockSpec((tm, d), lambda i: (i, gb_blk)),
                  head_spec, tail_spec,
                  const((da, d)), const((da, d)), const((d, d)), const((1, d)), const((1, d))],
        out_specs=[pl.BlockSpec((tm, d), lambda i: (i, 0)),
                   pl.BlockSpec((d, tm), lambda i: (0, i))],
        out_shape=[jax.ShapeDtypeStruct((n_tok, d), F32),
                   jax.ShapeDtypeStruct((d, n_tok), BF16)],
        compiler_params=_params(("parallel",), 58),
        name="merge_ln",
    )(ya, yb, z, z, xs[0], xs[1], wpa, wpb, wo, g, b)


def _compare_exchange(v, i, l):
    v[i], v[l] = jnp.maximum(v[i], v[l]), jnp.minimum(v[i], v[l])


def _bitonic_merge(v):
    n = len(v)
    j = n // 2
    while j:
        for i in range(n):
            if i ^ j > i:
                _compare_exchange(v, i, i ^ j)
        j //= 2
    return v


def _top_sorted(a, keep=PEER_TOPK):
    sub = 8
    n = a.shape[0] // sub
    assert n & (n - 1) == 0 and 2 * n >= keep
    v = [a[sub * k:sub * (k + 1)] for k in range(n)]
    k = 2
    while k <= n:
        j = k // 2
        while j:
            for i in range(n):
                l = i ^ j
                if l > i:
                    if i & k == 0:
                        _compare_exchange(v, i, l)
                    else:
                        _compare_exchange(v, l, i)
            j //= 2
        k *= 2
    shift = sub // 2
    while shift:
        other = [pltpu.roll(t, shift, 0) for t in v]
        m = len(v)
        if m < keep:
            v = _bitonic_merge(v + other[::-1])
        else:
            v = _bitonic_merge([jnp.maximum(v[i], other[m - 1 - i]) for i in range(m)])
        shift //= 2
    return [t[0:1] for t in v[:keep]]


def _stack_rows(rows, n):
    lanes = rows[0].shape[1]
    idx = lax.broadcasted_iota(jnp.int32, (n, lanes), 0)
    acc = jnp.broadcast_to(rows[0], (n, lanes))
    for r in range(1, n):
        acc = jnp.where(idx == r, rows[r], acc)
    return acc


def _peer_route_kernel(xt_ref, wqt_ref, sk_ref, thr_ref, c_ref, s2_ref, e2_ref, q_ref):
    q_ref[...] = jnp.dot(wqt_ref[...], xt_ref[...], preferred_element_type=F32).astype(BF16)
    half = sk_ref.shape[3]

    def per_head(h, carry):
        base = pl.multiple_of(h * 2 * half, 2 * half)
        s1 = jnp.dot(sk_ref[h, 0], q_ref[pl.ds(base, half), :], preferred_element_type=F32)
        s2 = jnp.dot(sk_ref[h, 1], q_ref[pl.ds(base + half, half), :], preferred_element_type=F32)
        t1 = _top_sorted(s1)
        t2 = _top_sorted(s2)
        t2_all = _stack_rows(t2, PEER_TOPK)
        t2_half = t2_all[:PEER_TOPK // 2]
        sums = [t1[a] + t2[b] for a in range(3, PEER_TOPK) for b in range(PEER_TOPK // (a + 1))]
        sums += [jnp.full_like(t1[0], -jnp.inf)] * (-len(sums) % 8)
        packed = [_stack_rows(sums[r:r + 8], 8) for r in range(0, len(sums), 8)]
        filler = jnp.full_like(t2_half, -jnp.inf)
        cand = jnp.concatenate([t1[0] + t2_all, t1[1] + t2_half, t1[2] + t2_half] + packed + [filler], axis=0)
        tau = _top_sorted(cand)[PEER_TOPK - 1]
        rows = [t1[a] + t2_all for a in range(PEER_TOPK)]
        count = lambda mask: jnp.sum(jnp.where(mask, 1.0, 0.0), axis=0, keepdims=True)
        n_gt = [count(r > tau) for r in rows]
        n_eq = [count(r == tau) for r in rows]
        spare = float(PEER_TOPK) - sum(n_gt)
        rank = lax.broadcasted_iota(jnp.int32, t2_all.shape, 0).astype(F32)
        top_score = t1[0] + t2[0]
        z = jnp.zeros_like(tau)
        thr = jnp.full(s1.shape, jnp.inf, F32)
        for a in range(PEER_TOPK):
            take = jnp.minimum(n_eq[a], spare)
            spare = spare - take
            keep = rank < n_gt[a] + take
            z = z + jnp.sum(jnp.where(keep, jnp.exp(rows[a] - top_score), 0.0), axis=0, keepdims=True)
            lowest_kept = jnp.min(jnp.where(keep, t2_all, jnp.inf), axis=0, keepdims=True)
            thr = jnp.where(s1 == t1[a], lowest_kept, thr)
        thr_ref[h] = thr
        c_ref[h] = jnp.exp(s1 - t1[0]) / z
        s2_ref[h] = s2
        e2_ref[h] = jnp.exp(s2 - t2[0])
        return carry

    lax.fori_loop(0, PEER_HEADS, per_head, 0)


def _peer_route(xt, wqt, sk, tb):
    d, n_tok = xt.shape
    nk = sk.shape[2]
    out = jax.ShapeDtypeStruct((PEER_HEADS, nk, n_tok), F32)
    ospec = pl.BlockSpec((PEER_HEADS, nk, tb), lambda i: (0, 0, i))
    return pl.pallas_call(
        _peer_route_kernel,
        grid=(n_tok // tb,),
        in_specs=[pl.BlockSpec((d, tb), lambda i: (0, i)),
                  pl.BlockSpec(wqt.shape, lambda i: (0, 0)),
                  pl.BlockSpec(sk.shape, lambda i: (0, 0, 0, 0))],
        out_specs=[ospec] * 4,
        out_shape=[out] * 4,
        scratch_shapes=[pltpu.VMEM((wqt.shape[0], tb), BF16)],
        compiler_params=_params(("parallel",), 48),
        name="peer_route",
    )(xt, wqt, sk)


def _peer_dense_kernel(n_tiles, n_exp_tiles, lane_chunk, xt_ref, u_ref, vt_ref, thr_ref, c_ref,
                       s2_ref, e2_ref, out_ref, ht_a, ht_b, pt_a, pt_b):
    s = pl.program_id(0)
    retrieved = jnp.clip(s - 2, 0, n_tiles - 1)

    @pl.when(s == 0)
    def _():
        for ref in (ht_a, ht_b, pt_a, pt_b):
            ref[...] = jnp.zeros_like(ref)

    @pl.when(retrieved % n_exp_tiles == 0)
    def _():
        out_ref[...] = jnp.zeros_like(out_ref)

    nk = s2_ref.shape[1]
    tb = xt_ref.shape[1]
    n_keys = u_ref.shape[0] // nk
    sub = 16

    def stages(ht_new, ht_old, pt_new, pt_old):
        def project(rs, ls):
            ht_new[rs, ls] = jnp.dot(u_ref[rs, :], xt_ref[:, ls], preferred_element_type=F32)

        def gate(keys, j0, ls):
            acc = {}
            for h in range(PEER_HEADS):
                s2 = s2_ref[h, j0:j0 + sub, ls]
                e2 = e2_ref[h, j0:j0 + sub, ls]
                for ii in keys:
                    hit = s2 >= thr_ref[h, ii:ii + 1, ls]
                    term = jnp.where(hit, e2, 0.0) * c_ref[h, ii:ii + 1, ls]
                    acc[ii] = term if h == 0 else acc[ii] + term
            for ii in keys:
                rs = slice(ii * nk + j0, ii * nk + j0 + sub)
                pt_new[rs, ls] = (acc[ii] * _gelu(ht_old[rs, ls])).astype(BF16)

        def retrieve(rs, ls):
            out_ref[rs, ls] += jnp.dot(vt_ref[rs, :], pt_old[:, ls], preferred_element_type=F32)

        lanes = [slice(t0, t0 + lane_chunk) for t0 in range(0, tb, lane_chunk)]
        blk = 256
        p_units = [functools.partial(project, slice(r0, r0 + blk), ls)
                   for ls in lanes for r0 in range(0, u_ref.shape[0], blk)]
        r_units = [functools.partial(retrieve, slice(r0, r0 + blk), ls)
                   for ls in lanes for r0 in range(0, vt_ref.shape[0], blk)]
        key_groups = [range(n_keys)]
        g_units = [functools.partial(gate, keys, j0, slice(g0, g0 + 128))
                   for g0 in range(0, tb, 128) for j0 in range(0, nk, sub) for keys in key_groups]
        for n, g_unit in enumerate(g_units):
            for units in (p_units, r_units):
                if n * len(units) % len(g_units) == 0:
                    units[n * len(units) // len(g_units)]()
            g_unit()

    @pl.when(s % 2 == 0)
    def _():
        stages(ht_a, ht_b, pt_b, pt_a)

    @pl.when(s % 2 == 1)
    def _():
        stages(ht_b, ht_a, pt_a, pt_b)


def _table_prep_kernel(u_ref, v_ref, ub_ref, vt_ref):
    ub_ref[...] = u_ref[...].astype(BF16)
    vt_ref[...] = v_ref[...].T.astype(BF16)


def _table_prep(peer_u, peer_v, eb):
    depth, n_exp, d = peer_u.shape
    in_spec = pl.BlockSpec((None, eb, d), lambda l, e: (l, e, 0))
    return pl.pallas_call(
        _table_prep_kernel,
        grid=(depth, n_exp // eb),
        in_specs=[in_spec, in_spec],
        out_specs=[pl.BlockSpec((None, eb, d), lambda l, e: (l, e, 0)),
                   pl.BlockSpec((None, d, eb), lambda l, e: (l, 0, e))],
        out_shape=[jax.ShapeDtypeStruct((depth, n_exp, d), BF16),
                   jax.ShapeDtypeStruct((depth, d, n_exp), BF16)],
        compiler_params=_params(("parallel", "parallel"), 48),
        name="table_prep",
    )(peer_u, peer_v)


def _peer_dense(layer, xt, u, vt, thr, c, s2, e2, tb, eb):
    d, n_tok = xt.shape
    n_exp = u.shape[1]
    nk = s2.shape[1]
    keys_per_blk = eb // nk
    n_e = n_exp // eb
    n_tiles = (n_tok // tb) * n_e
    proj = lambda s: jnp.minimum(s, n_tiles - 1)
    gate = lambda s: jnp.clip(s - 1, 0, n_tiles - 1)
    retr = lambda s: jnp.clip(s - 2, 0, n_tiles - 1)
    row_spec = pl.BlockSpec((PEER_HEADS, keys_per_blk, tb), lambda s: (0, gate(s) % n_e, gate(s) // n_e))
    all_spec = pl.BlockSpec((PEER_HEADS, nk, tb), lambda s: (0, 0, gate(s) // n_e))
    return pl.pallas_call(
        functools.partial(_peer_dense_kernel, n_tiles, n_e, 256),
        grid=(n_tiles + 2,),
        in_specs=[pl.BlockSpec((d, tb), lambda s: (0, proj(s) // n_e)),
                  pl.BlockSpec((None, eb, d), lambda s: (layer, proj(s) % n_e, 0)),
                  pl.BlockSpec((None, d, eb), lambda s: (layer, 0, retr(s) % n_e)),
                  row_spec, row_spec, all_spec, all_spec],
        out_specs=pl.BlockSpec((d, tb), lambda s: (0, retr(s) // n_e)),
        out_shape=jax.ShapeDtypeStruct((d, n_tok), F32),
        scratch_shapes=[pltpu.VMEM((eb, tb), F32), pltpu.VMEM((eb, tb), F32),
                        pltpu.VMEM((eb, tb), BF16), pltpu.VMEM((eb, tb), BF16)],
        compiler_params=_params(("arbitrary",), 58),
        name="peer_dense",
    )(xt, u, vt, thr, c, s2, e2)


def _residual_ln_kernel(alpha, x_ref, pt_ref, g_ref, b_ref, o_ref, ob_ref):
    y = _layer_norm(alpha * x_ref[...] + pt_ref[...].T, g_ref[...], b_ref[...])
    o_ref[...] = y
    ob_ref[...] = y.astype(BF16)


def _residual_ln(alpha, x, pt, g, b, tm, row0=0, n_tok=None):
    d = x.shape[1]
    n_tok = x.shape[0] if n_tok is None else n_tok
    blk0 = row0 // tm
    return pl.pallas_call(
        functools.partial(_residual_ln_kernel, alpha),
        grid=(n_tok // tm,),
        in_specs=[pl.BlockSpec((tm, d), lambda i: (blk0 + i, 0)),
                  pl.BlockSpec((d, tm), lambda i: (0, blk0 + i)),
                  pl.BlockSpec((1, d), lambda i: (0, 0)),
                  pl.BlockSpec((1, d), lambda i: (0, 0))],
        out_specs=[pl.BlockSpec((tm, d), lambda i: (i, 0))] * 2,
        out_shape=[jax.ShapeDtypeStruct((n_tok, d), F32), jax.ShapeDtypeStruct((n_tok, d), BF16)],
        compiler_params=_params(("parallel",), 40),
        name="residual_ln",
    )(x, pt, g, b)


def _block_diag(w, reps):
    return jnp.kron(jnp.eye(reps, dtype=w.dtype), w)


def kernel(x_prompt, x_sample, state_hgrn, w_in, sgu_ln_g, sgu_ln_b, sgu_w, sgu_b, hg_lb_logits,
           hg_norm_g, w_branch_a, w_branch_b, w_out, ln1_g, ln1_b, peer_w_q, peer_sub_keys,
           peer_u, peer_v, ln2_g, ln2_b):
    depth = w_in.shape[0]
    n_b, seq, d = x_prompt.shape
    n_s, dec_seq, _ = x_sample.shape
    width = sgu_ln_g.shape[1]
    n_prompt = n_b * seq
    alpha = (2 * depth) ** 0.25
    assert seq % ROWS == 0 and ROWS % dec_seq == 0 and n_s % (ROWS // dec_seq) == 0
    assert w_in.shape[2] == 6 * width + 2 * d and d == 2 * width
    assert n_prompt % 1024 == 0 and (n_s * dec_seq) % 1024 == 0

    xs = (x_prompt.reshape(n_prompt, d), x_sample.reshape(n_s * dec_seq, d))
    xs_b = tuple(a.astype(BF16) for a in xs)
    row = lambda a: a.reshape(1, -1)

    tri = jnp.tril(jnp.ones((ROWS, ROWS), F32))
    tri_s = _block_diag(jnp.tril(jnp.ones((dec_seq, dec_seq), F32)), ROWS // dec_seq)

    u_tab, vt_tab = _table_prep(peer_u, peer_v, 512)

    new_sp, new_ss, new_v = [], [], []
    for l in range(depth):
        z = _matmul(xs_b, w_in, l, 1024, 1024, "in_proj")

        ws = jnp.where(tri > 0, sgu_w[l], 0.0)
        ws_s = jax.vmap(lambda w: _block_diag(w, ROWS // dec_seq))(ws[:, :dec_seq, :dec_seq])
        bias = jnp.repeat(sgu_b[l].T, width // SGU_GROUPS, axis=1)
        bias_s = jnp.tile(bias[:dec_seq], (ROWS // dec_seq, 1))
        common = (row(sgu_ln_g[l]), row(sgu_ln_b[l]))
        tail = (hg_lb_logits, row(hg_norm_g[l]))
        ya_p, yb_p, st_p = _mixer_prompt(l, z, (n_b, seq), width, *common, ws.astype(BF16), bias,
                                         tri.astype(BF16), *tail)
        ya_s, yb_s, vn_s, st_s = _mixer_sample(l, z, n_prompt, (n_s, dec_seq), width, *common,
                                               ws_s.astype(BF16), bias_s, tri_s.astype(BF16), *tail,
                                               state_hgrn)
        ya = jnp.concatenate([ya_p, ya_s], axis=0)
        yb = jnp.concatenate([yb_p, yb_s], axis=0)

        x1, x1t = _merge(alpha, ya, yb, z, xs, w_branch_a[l].astype(BF16), w_branch_b[l].astype(BF16),
                         w_out[l].astype(BF16), row(ln1_g[l]), row(ln1_b[l]), 256,
                         (6 * width // d, 6 * width // d + 1))

        thr, c, s2, e2 = _peer_route(x1t, peer_w_q[l].T.astype(BF16), peer_sub_keys[l].astype(BF16), 256)
        pt = _peer_dense(l, x1t, u_tab, vt_tab, thr, c, s2, e2, 512, 1024)
        ln2 = (row(ln2_g[l]), row(ln2_b[l]), 512)
        y_p, yb_p = _residual_ln(alpha, x1, pt, *ln2, 0, n_prompt)
        y_s, yb_s = _residual_ln(alpha, x1, pt, *ln2, n_prompt, n_s * dec_seq)
        xs, xs_b = (y_p, y_s), (yb_p, yb_s)

        new_sp.append(st_p)
        new_ss.append(st_s)
        new_v.append(vn_s.reshape(n_s, dec_seq, width))

    return (xs[0].reshape(n_b, seq, d), xs[1].reshape(n_s, dec_seq, d),
            jnp.stack(new_sp), jnp.stack(new_ss), jnp.stack(new_v))
```

```python
import functools

import jax
import jax.numpy as jnp
from jax import lax
from jax.experimental import pallas as pl
from jax.experimental.pallas import tpu as pltpu

F32 = jnp.float32
BF16 = jnp.bfloat16

SGU_GROUPS = 4
HG_HEADS = 8
PEER_HEADS = 8
PEER_TOPK = 16
LN_EPS = 1e-5
RMS_EPS = 1e-6
ROWS = 128
HG_SUB = 16
MIB = 1024 * 1024

_NT = (((1,), (1,)), ((), ()))
_TN = (((0,), (0,)), ((), ()))


def _params(sem, vmem_mib):
    return pltpu.CompilerParams(dimension_semantics=sem, vmem_limit_bytes=vmem_mib * MIB)


def _gelu(x):
    c = 0.7978845608028654
    half = 0.5 * x
    return half + half * jnp.tanh(x * (c + (c * 0.044715) * (x * x)))


def _sigmoid(x):
    return 1.0 / (1.0 + jnp.exp(-x))


def _layer_norm(x, g, b):
    mu = jnp.mean(x, axis=-1, keepdims=True)
    xc = x - mu
    var = jnp.mean(xc * xc, axis=-1, keepdims=True)
    return xc * lax.rsqrt(var + LN_EPS) * g + b


def _bdot(a, b, dims=None):
    a = a.astype(BF16)
    b = b.astype(BF16)
    if dims is None:
        return jnp.dot(a, b, preferred_element_type=F32)
    return lax.dot_general(a, b, dims, preferred_element_type=F32)


def _two_group_specs(block, head_rows, tile, grid_axis):
    n_head = head_rows // tile
    pick = lambda ids: ids[grid_axis]
    return (n_head,
            pl.BlockSpec(block, lambda *ids: (jnp.minimum(pick(ids), n_head - 1), 0)),
            pl.BlockSpec(block, lambda *ids: (jnp.maximum(pick(ids) - n_head, 0), 0)))


def _matmul_kernel(n_head, xh_ref, xt_ref, w_ref, o_ref, wb_ref):
    i = pl.program_id(1)

    @pl.when(i == 0)
    def _():
        wb_ref[...] = w_ref[...].astype(BF16)

    @pl.when(i < n_head)
    def _():
        o_ref[...] = jnp.dot(xh_ref[...], wb_ref[...], preferred_element_type=F32)

    @pl.when(i >= n_head)
    def _():
        o_ref[...] = jnp.dot(xt_ref[...], wb_ref[...], preferred_element_type=F32)


def _matmul(xs, w, layer, tm, tn, name):
    k = xs[0].shape[1]
    m = xs[0].shape[0] + xs[1].shape[0]
    n = w.shape[2]
    n_head, head_spec, tail_spec = _two_group_specs((tm, k), xs[0].shape[0], tm, 1)
    return pl.pallas_call(
        functools.partial(_matmul_kernel, n_head),
        grid=(n // tn, m // tm),
        in_specs=[head_spec, tail_spec,
                  pl.BlockSpec((None, k, tn), lambda j, i: (layer, 0, j))],
        out_specs=pl.BlockSpec((tm, tn), lambda j, i: (i, j)),
        out_shape=jax.ShapeDtypeStruct((m, n), F32),
        scratch_shapes=[pltpu.VMEM((k, tn), BF16)],
        compiler_params=_params(("parallel", "arbitrary"), 56),
        name=name,
    )(xs[0], xs[1], w)


def _forget_lower_bound(logits_ref, layer):
    lg = logits_ref[...]
    e = jnp.exp(lg - jnp.max(lg, axis=0, keepdims=True))
    p = e / jnp.sum(e, axis=0, keepdims=True)
    return jnp.sum(p[1:layer + 1], axis=0, keepdims=True)


def _hgrn_gates(fp, lb):
    e = jnp.exp(-jnp.abs(fp))
    log_sig = jnp.minimum(fp, 0.0) - jnp.log1p(e)
    sig_neg = jnp.where(fp >= 0.0, e, 1.0) / (1.0 + e)
    if lb is None:
        return log_sig, sig_neg
    k = (1.0 - lb) * sig_neg
    pos = lb > 0.0
    a = jnp.log(jnp.where(pos, lb, 1.0))
    b = jnp.log1p(-lb) + log_sig
    lse = jnp.maximum(a, b) + jnp.log1p(jnp.exp(-jnp.abs(a - b)))
    return jnp.where(pos, lse, b), k


def _segment_cumsum(t_ref, x):
    t = t_ref[...]
    hi = x.astype(BF16)
    r1 = x - hi.astype(F32)
    mid = r1.astype(BF16)
    lo = (r1 - mid.astype(F32)).astype(BF16)
    dot = functools.partial(jnp.dot, preferred_element_type=F32)
    return dot(t, hi) + dot(t, mid) + dot(t, lo)


def _sgu(u, v, lng_ref, lnb_ref, ws_ref, bias_ref, ya_ref):
    vn = _layer_norm(_gelu(v), lng_ref[...], lnb_ref[...])
    vnb = vn.astype(BF16)
    gw = vn.shape[1] // SGU_GROUPS
    for g in range(SGU_GROUPS):
        cs = slice(g * gw, (g + 1) * gw)
        mixed = jnp.dot(ws_ref[g], vnb[:, cs], preferred_element_type=F32) + bias_ref[:, cs]
        ya_ref[:, cs] = (_gelu(u[:, cs]) * mixed).astype(BF16)
    return vn


def _hgrn_out(o, gn, og):
    o = o * lax.rsqrt(jnp.mean(o * o, axis=-1, keepdims=True) + RMS_EPS)
    return (o * gn * (og * _sigmoid(og))).astype(BF16)


def _rows_at(b_ref, size, offset, width):
    parts = []
    for j in range(ROWS // size):
        r = j * size + offset
        if r < 0:
            parts.append(jnp.zeros((size, width), F32))
        else:
            parts.append(jnp.broadcast_to(b_ref[pl.ds(r, 1), :], (size, width)))
    return parts[0] if len(parts) == 1 else jnp.concatenate(parts, axis=0)


def _mixer_prompt_kernel(layer, u_ref, v_ref, q_ref, f_ref, i_ref, og_ref, lng_ref, lnb_ref,
                         ws_ref, bias_ref, tri_ref, lbl_ref, gn_ref,
                         ya_ref, yb_ref, st_ref, stt_ref, b_ref):
    c = pl.program_id(1)
    width = q_ref.shape[1]
    dk = width // HG_HEADS

    @pl.when(c == 0)
    def _():
        stt_ref[...] = jnp.zeros_like(stt_ref)

    _sgu(u_ref[...], v_ref[...], lng_ref, lnb_ref, ws_ref, bias_ref, ya_ref)

    lb = None if layer == 0 else _forget_lower_bound(lbl_ref, layer)
    lf, k = _hgrn_gates(f_ref[...], lb)
    qs = q_ref[...] * (dk ** -0.5)
    b = _segment_cumsum(tri_ref, lf)
    b_ref[...] = b

    e_sub = b - _rows_at(b_ref, HG_SUB, -1, width)
    q_lvl = [qs * jnp.exp(e_sub)]
    k_lvl = [k * jnp.exp(-e_sub)]
    size = HG_SUB
    sizes = []
    while size < ROWS:
        sizes.append(size)
        q_lvl.append(q_lvl[0] if size == HG_SUB else qs * jnp.exp(b - _rows_at(b_ref, size, -1, width)))
        k_lvl.append(k * jnp.exp(_rows_at(b_ref, size, size - 1, width) - b))
        size *= 2
    qd = qs * jnp.exp(b)
    b_end = b_ref[pl.ds(ROWS - 1, 1), :]
    ke = k * jnp.exp(b_end - b)
    g_end = jnp.exp(b_end)

    row = lax.broadcasted_iota(jnp.int32, (ROWS, ROWS), 0)
    col = lax.broadcasted_iota(jnp.int32, (ROWS, ROWS), 1)
    masks = [(row // HG_SUB == col // HG_SUB) & (col <= row)]
    for sz in sizes:
        masks.append(((row // sz) % 2 == 1) & (col // sz == row // sz - 1))

    vv = i_ref[...]
    gn = gn_ref[...]
    og = og_ref[...]
    for h in range(HG_HEADS):
        hs = slice(h * dk, (h + 1) * dk)
        attn = jnp.zeros((ROWS, ROWS), F32)
        for ql, kl, mk in zip(q_lvl, k_lvl, masks):
            attn = jnp.where(mk, _bdot(ql[:, hs], kl[:, hs], _NT), attn)
        vh = vv[:, hs]
        stt = stt_ref[h]
        o = _bdot(attn, vh) + _bdot(qd[:, hs], stt, _NT)
        stt_ref[h] = stt * g_end[:, hs] + _bdot(vh, ke[:, hs], _TN)
        yb_ref[:, hs] = _hgrn_out(o, gn[:, hs], og[:, hs])

    @pl.when(c == pl.num_programs(1) - 1)
    def _():
        for h in range(HG_HEADS):
            st_ref[0, h] = stt_ref[h].T


def _mixer_sample_kernel(layer, seq_len, u_ref, v_ref, q_ref, f_ref, i_ref, og_ref, lng_ref, lnb_ref,
                         ws_ref, bias_ref, tri_ref, lbl_ref, gn_ref, s0_ref,
                         ya_ref, yb_ref, vn_ref, st_ref, b_ref, oi_ref):
    width = q_ref.shape[1]
    dk = width // HG_HEADS
    n_seq = ROWS // seq_len

    vn_ref[...] = _sgu(u_ref[...], v_ref[...], lng_ref, lnb_ref, ws_ref, bias_ref, ya_ref)

    lb = None if layer == 0 else _forget_lower_bound(lbl_ref, layer)
    lf, k = _hgrn_gates(f_ref[...], lb)
    qs = q_ref[...] * (dk ** -0.5)
    b = _segment_cumsum(tri_ref, lf)
    b_ref[...] = b
    qd = qs * jnp.exp(b)
    k0 = k * jnp.exp(-b)
    ke = k * jnp.exp(_rows_at(b_ref, seq_len, seq_len - 1, width) - b)

    row = lax.broadcasted_iota(jnp.int32, (ROWS, ROWS), 0)
    col = lax.broadcasted_iota(jnp.int32, (ROWS, ROWS), 1)
    causal = (row // seq_len == col // seq_len) & (col <= row)
    seq_of_row = lax.broadcasted_iota(jnp.int32, (ROWS, dk), 0) // seq_len

    vv = i_ref[...]
    gn = gn_ref[...]
    og = og_ref[...]
    for h in range(HG_HEADS):
        hs = slice(h * dk, (h + 1) * dk)
        attn = jnp.where(causal, _bdot(qd[:, hs], k0[:, hs], _NT), 0.0)
        vh = vv[:, hs]
        qdh = qd[:, hs].astype(BF16)
        keh = ke[:, hs].astype(BF16)
        oi_ref[...] = _bdot(attn, vh)

        def per_seq(s, carry, h=h, hs=hs, vh=vh, qdh=qdh, keh=keh):
            mine = seq_of_row == s
            s0 = s0_ref[s, h]
            oi_ref[...] += jnp.where(mine, _bdot(qdh, s0), 0.0)
            b_seq = b_ref[pl.ds(pl.multiple_of(s * seq_len, seq_len), seq_len), hs]
            g_end = jnp.exp(b_seq[seq_len - 1:seq_len])
            new_t = s0.T * g_end + _bdot(jnp.where(mine, vh, 0.0), keh, _TN)
            st_ref[s, h] = new_t.T
            return carry

        lax.fori_loop(0, n_seq, per_seq, 0, unroll=4)
        yb_ref[:, hs] = _hgrn_out(oi_ref[...], gn[:, hs], og[:, hs])


def _mixer_specs(width, row_block, n_axes):
    def idx(f):
        return f if n_axes == 2 else (lambda t: f(t, 0))

    def zcol(j):
        return pl.BlockSpec((ROWS, width), idx(lambda a, c, j=j: (row_block(a, c), j)))

    def const(shape):
        return pl.BlockSpec(shape, idx(lambda a, c: (0,) * len(shape)))

    return zcol, const


def _mixer_prompt(layer, z, batch, width, lng, lnb, ws, bias, tri, lbl, gn):
    n_chunks = batch[1] // ROWS
    zcol, const = _mixer_specs(width, lambda a, c: a * n_chunks + c, 2)
    depth = lbl.shape[0]
    dk = width // HG_HEADS
    n_tok = batch[0] * batch[1]
    return pl.pallas_call(
        functools.partial(_mixer_prompt_kernel, layer),
        grid=(batch[0], n_chunks),
        in_specs=[zcol(j) for j in range(6)] + [
            const((1, width)), const((1, width)), const((SGU_GROUPS, ROWS, ROWS)),
            const((ROWS, width)), const((ROWS, ROWS)), const((depth, width)), const((1, width))],
        out_specs=[pl.BlockSpec((ROWS, width), lambda a, c: (a * n_chunks + c, 0)),
                   pl.BlockSpec((ROWS, width), lambda a, c: (a * n_chunks + c, 0)),
                   pl.BlockSpec((1, HG_HEADS, dk, dk), lambda a, c: (a, 0, 0, 0))],
        out_shape=[jax.ShapeDtypeStruct((n_tok, width), BF16),
                   jax.ShapeDtypeStruct((n_tok, width), BF16),
                   jax.ShapeDtypeStruct((batch[0], HG_HEADS, dk, dk), F32)],
        scratch_shapes=[pltpu.VMEM((HG_HEADS, dk, dk), F32), pltpu.VMEM((ROWS, width), F32)],
        compiler_params=_params(("parallel", "arbitrary"), 56),
        name="mixer_prompt",
    )(z, z, z, z, z, z, lng, lnb, ws, bias, tri, lbl, gn)


def _mixer_sample(layer, z, row0, batch, width, lng, lnb, ws, bias, tri, lbl, gn, s0):
    n_seq, seq_len = batch
    per_step = ROWS // seq_len
    steps = n_seq // per_step
    blk0 = row0 // ROWS
    zcol, const = _mixer_specs(width, lambda t, c: blk0 + t, 1)
    depth = lbl.shape[0]
    dk = width // HG_HEADS
    n_tok = n_seq * seq_len
    st_spec = pl.BlockSpec((per_step, HG_HEADS, dk, dk), lambda t: (t, 0, 0, 0))
    row_spec = pl.BlockSpec((ROWS, width), lambda t: (t, 0))
    return pl.pallas_call(
        functools.partial(_mixer_sample_kernel, layer, seq_len),
        grid=(steps,),
        in_specs=[zcol(j) for j in range(6)] + [
            const((1, width)), const((1, width)), const((SGU_GROUPS, ROWS, ROWS)),
            const((ROWS, width)), const((ROWS, ROWS)), const((depth, width)), const((1, width)),
            pl.BlockSpec((None, per_step, HG_HEADS, dk, dk), lambda t: (layer, t, 0, 0, 0))],
        out_specs=[row_spec, row_spec, row_spec, st_spec],
        out_shape=[jax.ShapeDtypeStruct((n_tok, width), BF16),
                   jax.ShapeDtypeStruct((n_tok, width), BF16),
                   jax.ShapeDtypeStruct((n_tok, width), F32),
                   jax.ShapeDtypeStruct((n_seq, HG_HEADS, dk, dk), F32)],
        scratch_shapes=[pltpu.VMEM((ROWS, width), F32), pltpu.VMEM((ROWS, dk), F32)],
        compiler_params=_params(("parallel",), 58),
        name="mixer_sample",
    )(z, z, z, z, z, z, lng, lnb, ws, bias, tri, lbl, gn, s0)


def _merge_kernel(alpha, n_head, ya_ref, yb_ref, ga_ref, gb_ref, xh_ref, xt_ref, wpa_ref, wpb_ref, wo_ref,
                  g_ref, b_ref, x1_ref, x1t_ref):
    a = jnp.dot(ya_ref[...], wpa_ref[...], preferred_element_type=F32)
    b = jnp.dot(yb_ref[...], wpb_ref[...], preferred_element_type=F32)
    merged = _sigmoid(ga_ref[...]) * a + _sigmoid(gb_ref[...]) * b
    y = jnp.dot(merged.astype(BF16), wo_ref[...], preferred_element_type=F32)
    x = jnp.where(pl.program_id(0) < n_head, xh_ref[...], xt_ref[...])
    x1 = _layer_norm(alpha * x + y, g_ref[...], b_ref[...])
    x1_ref[...] = x1
    x1t_ref[...] = x1.T.astype(BF16)


def _merge(alpha, ya, yb, z, xs, wpa, wpb, wo, g, b, tm, gate_blocks):
    d = xs[0].shape[1]
    n_tok = xs[0].shape[0] + xs[1].shape[0]
    da = ya.shape[1]
    ga_blk, gb_blk = gate_blocks
    const = lambda shape: pl.BlockSpec(shape, lambda i: (0,) * len(shape), pipeline_mode=pl.Buffered(1))
    n_head, head_spec, tail_spec = _two_group_specs((tm, d), xs[0].shape[0], tm, 0)
    return pl.pallas_call(
        functools.partial(_merge_kernel, alpha, n_head),
        grid=(n_tok // tm,),
        in_specs=[pl.BlockSpec((tm, da), lambda i: (i, 0)),
                  pl.BlockSpec((tm, da), lambda i: (i, 0)),
                  pl.BlockSpec((tm, d), lambda i: (i, ga_blk)),
                  pl.BlockSpec((tm, d), lambda i: (i, gb_blk)),
                  head_spec, tail_spec,
                  const((da, d)), const((da, d)), const((d, d)), const((1, d)), const((1, d))],
        out_specs=[pl.BlockSpec((tm, d), lambda i: (i, 0)),
                   pl.BlockSpec((d, tm), lambda i: (0, i))],
        out_shape=[jax.ShapeDtypeStruct((n_tok, d), F32),
                   jax.ShapeDtypeStruct((d, n_tok), BF16)],
        compiler_params=_params(("parallel",), 58),
        name="merge_ln",
    )(ya, yb, z, z, xs[0], xs[1], wpa, wpb, wo, g, b)


def _compare_exchange(v, i, l):
    v[i], v[l] = jnp.maximum(v[i], v[l]), jnp.minimum(v[i], v[l])


def _bitonic_merge(v):
    n = len(v)
    j = n // 2
    while j:
        for i in range(n):
            if i ^ j > i:
                _compare_exchange(v, i, i ^ j)
        j //= 2
    return v


def _top_sorted(a, keep=PEER_TOPK):
    sub = 8
    n = a.shape[0] // sub
    assert n & (n - 1) == 0 and 2 * n >= keep
    v = [a[sub * k:sub * (k + 1)] for k in range(n)]
    k = 2
    while k <= n:
        j = k // 2
        while j:
            for i in range(n):
                l = i ^ j
                if l > i:
                    if i & k == 0:
                        _compare_exchange(v, i, l)
                    else:
                        _compare_exchange(v, l, i)
            j //= 2
        k *= 2
    shift = sub // 2
    while shift:
        other = [pltpu.roll(t, shift, 0) for t in v]
        m = len(v)
        if m < keep:
            v = _bitonic_merge(v + other[::-1])
        else:
            v = _bitonic_merge([jnp.maximum(v[i], other[m - 1 - i]) for i in range(m)])
        shift //= 2
    return [t[0:1] for t in v[:keep]]


def _stack_rows(rows, n):
    lanes = rows[0].shape[1]
    idx = lax.broadcasted_iota(jnp.int32, (n, lanes), 0)
    acc = jnp.broadcast_to(rows[0], (n, lanes))
    for r in range(1, n):
        acc = jnp.where(idx == r, rows[r], acc)
    return acc


def _peer_route_kernel(xt_ref, wqt_ref, sk_ref, thr_ref, c_ref, s2_ref, e2_ref, q_ref):
    q_ref[...] = jnp.dot(wqt_ref[...], xt_ref[...], preferred_element_type=F32).astype(BF16)
    half = sk_ref.shape[3]

    def per_head(h, carry):
        base = pl.multiple_of(h * 2 * half, 2 * half)
        s1 = jnp.dot(sk_ref[h, 0], q_ref[pl.ds(base, half), :], preferred_element_type=F32)
        s2 = jnp.dot(sk_ref[h, 1], q_ref[pl.ds(base + half, half), :], preferred_element_type=F32)
        t1 = _top_sorted(s1)
        t2 = _top_sorted(s2)
        t2_all = _stack_rows(t2, PEER_TOPK)
        t2_half = t2_all[:PEER_TOPK // 2]
        sums = [t1[a] + t2[b] for a in range(3, PEER_TOPK) for b in range(PEER_TOPK // (a + 1))]
        sums += [jnp.full_like(t1[0], -jnp.inf)] * (-len(sums) % 8)
        packed = [_stack_rows(sums[r:r + 8], 8) for r in range(0, len(sums), 8)]
        filler = jnp.full_like(t2_half, -jnp.inf)
        cand = jnp.concatenate([t1[0] + t2_all, t1[1] + t2_half, t1[2] + t2_half] + packed + [filler], axis=0)
        tau = _top_sorted(cand)[PEER_TOPK - 1]
        rows = [t1[a] + t2_all for a in range(PEER_TOPK)]
        count = lambda mask: jnp.sum(jnp.where(mask, 1.0, 0.0), axis=0, keepdims=True)
        n_gt = [count(r > tau) for r in rows]
        n_eq = [count(r == tau) for r in rows]
        spare = float(PEER_TOPK) - sum(n_gt)
        rank = lax.broadcasted_iota(jnp.int32, t2_all.shape, 0).astype(F32)
        top_score = t1[0] + t2[0]
        z = jnp.zeros_like(tau)
        thr = jnp.full(s1.shape, jnp.inf, F32)
        for a in range(PEER_TOPK):
            take = jnp.minimum(n_eq[a], spare)
            spare = spare - take
            keep = rank < n_gt[a] + take
            z = z + jnp.sum(jnp.where(keep, jnp.exp(rows[a] - top_score), 0.0), axis=0, keepdims=True)
            lowest_kept = jnp.min(jnp.where(keep, t2_all, jnp.inf), axis=0, keepdims=True)
            thr = jnp.where(s1 == t1[a], lowest_kept, thr)
        thr_ref[h] = thr
        c_ref[h] = jnp.exp(s1 - t1[0]) / z
        s2_ref[h] = s2
        e2_ref[h] = jnp.exp(s2 - t2[0])
        return carry

    lax.fori_loop(0, PEER_HEADS, per_head, 0)


def _peer_route(xt, wqt, sk, tb):
    d, n_tok = xt.shape
    nk = sk.shape[2]
    out = jax.ShapeDtypeStruct((PEER_HEADS, nk, n_tok), F32)
    ospec = pl.BlockSpec((PEER_HEADS, nk, tb), lambda i: (0, 0, i))
    return pl.pallas_call(
        _peer_route_kernel,
        grid=(n_tok // tb,),
        in_specs=[pl.BlockSpec((d, tb), lambda i: (0, i)),
                  pl.BlockSpec(wqt.shape, lambda i: (0, 0)),
                  pl.BlockSpec(sk.shape, lambda i: (0, 0, 0, 0))],
        out_specs=[ospec] * 4,
        out_shape=[out] * 4,
        scratch_shapes=[pltpu.VMEM((wqt.shape[0], tb), BF16)],
        compiler_params=_params(("parallel",), 48),
        name="peer_route",
    )(xt, wqt, sk)


def _peer_dense_kernel(n_tiles, n_exp_tiles, lane_chunk, xt_ref, u_ref, vt_ref, thr_ref, c_ref,
                       s2_ref, e2_ref, out_ref, ht_a, ht_b, pt_a, pt_b):
    s = pl.program_id(0)
    retrieved = jnp.clip(s - 2, 0, n_tiles - 1)

    @pl.when(s == 0)
    def _():
        for ref in (ht_a, ht_b, pt_a, pt_b):
            ref[...] = jnp.zeros_like(ref)

    @pl.when(retrieved % n_exp_tiles == 0)
    def _():
        out_ref[...] = jnp.zeros_like(out_ref)

    nk = s2_ref.shape[1]
    tb = xt_ref.shape[1]
    n_keys = u_ref.shape[0] // nk
    sub = 16

    def stages(ht_new, ht_old, pt_new, pt_old):
        def project(rs, ls):
            ht_new[rs, ls] = jnp.dot(u_ref[rs, :], xt_ref[:, ls], preferred_element_type=F32)

        def gate(keys, j0, ls):
            acc = {}
            for h in range(PEER_HEADS):
                s2 = s2_ref[h, j0:j0 + sub, ls]
                e2 = e2_ref[h, j0:j0 + sub, ls]
                for ii in keys:
                    hit = s2 >= thr_ref[h, ii:ii + 1, ls]
                    term = jnp.where(hit, e2, 0.0) * c_ref[h, ii:ii + 1, ls]
                    acc[ii] = term if h == 0 else acc[ii] + term
            for ii in keys:
                rs = slice(ii * nk + j0, ii * nk + j0 + sub)
                pt_new[rs, ls] = (acc[ii] * _gelu(ht_old[rs, ls])).astype(BF16)

        def retrieve(rs, ls):
            out_ref[rs, ls] += jnp.dot(vt_ref[rs, :], pt_old[:, ls], preferred_element_type=F32)

        lanes = [slice(t0, t0 + lane_chunk) for t0 in range(0, tb, lane_chunk)]
        blk = 128
        p_units = [functools.partial(project, slice(r0, r0 + blk), ls)
                   for ls in lanes for r0 in range(0, u_ref.shape[0], blk)]
        r_units = [functools.partial(retrieve, slice(r0, r0 + blk), ls)
                   for ls in lanes for r0 in range(0, vt_ref.shape[0], blk)]
        key_groups = [range(n_keys)]
        g_units = [functools.partial(gate, keys, j0, slice(g0, g0 + 128))
                   for g0 in range(0, tb, 128) for j0 in range(0, nk, sub) for keys in key_groups]
        for n, g_unit in enumerate(g_units):
            for units in (p_units, r_units):
                if n * len(units) % len(g_units) == 0:
                    units[n * len(units) // len(g_units)]()
            g_unit()

    @pl.when(s % 2 == 0)
    def _():
        stages(ht_a, ht_b, pt_b, pt_a)

    @pl.when(s % 2 == 1)
    def _():
        stages(ht_b, ht_a, pt_a, pt_b)


def _table_prep_kernel(u_ref, v_ref, ub_ref, vt_ref):
    ub_ref[...] = u_ref[...].astype(BF16)
    vt_ref[...] = v_ref[...].T.astype(BF16)


def _table_prep(peer_u, peer_v, eb):
    depth, n_exp, d = peer_u.shape
    in_spec = pl.BlockSpec((None, eb, d), lambda l, e: (l, e, 0))
    return pl.pallas_call(
        _table_prep_kernel,
        grid=(depth, n_exp // eb),
        in_specs=[in_spec, in_spec],
        out_specs=[pl.BlockSpec((None, eb, d), lambda l, e: (l, e, 0)),
                   pl.BlockSpec((None, d, eb), lambda l, e: (l, 0, e))],
        out_shape=[jax.ShapeDtypeStruct((depth, n_exp, d), BF16),
                   jax.ShapeDtypeStruct((depth, d, n_exp), BF16)],
        compiler_params=_params(("parallel", "parallel"), 48),
        name="table_prep",
    )(peer_u, peer_v)


def _peer_dense(layer, xt, u, vt, thr, c, s2, e2, tb, eb):
    d, n_tok = xt.shape
    n_exp = u.shape[1]
    nk = s2.shape[1]
    keys_per_blk = eb // nk
    n_e = n_exp // eb
    n_tiles = (n_tok // tb) * n_e
    proj = lambda s: jnp.minimum(s, n_tiles - 1)
    gate = lambda s: jnp.clip(s - 1, 0, n_tiles - 1)
    retr = lambda s: jnp.clip(s - 2, 0, n_tiles - 1)
    row_spec = pl.BlockSpec((PEER_HEADS, keys_per_blk, tb), lambda s: (0, gate(s) % n_e, gate(s) // n_e))
    all_spec = pl.BlockSpec((PEER_HEADS, nk, tb), lambda s: (0, 0, gate(s) // n_e))
    return pl.pallas_call(
        functools.partial(_peer_dense_kernel, n_tiles, n_e, 256),
        grid=(n_tiles + 2,),
        in_specs=[pl.BlockSpec((d, tb), lambda s: (0, proj(s) // n_e)),
                  pl.BlockSpec((None, eb, d), lambda s: (layer, proj(s) % n_e, 0)),
                  pl.BlockSpec((None, d, eb), lambda s: (layer, 0, retr(s) % n_e)),
                  row_spec, row_spec, all_spec, all_spec],
        out_specs=pl.BlockSpec((d, tb), lambda s: (0, retr(s) // n_e)),
        out_shape=jax.ShapeDtypeStruct((d, n_tok), F32),
        scratch_shapes=[pltpu.VMEM((eb, tb), F32), pltpu.VMEM((eb, tb), F32),
                        pltpu.VMEM((eb, tb), BF16), pltpu.VMEM((eb, tb), BF16)],
        compiler_params=_params(("arbitrary",), 58),
        name="peer_dense",
    )(xt, u, vt, thr, c, s2, e2)


def _residual_ln_kernel(alpha, x_ref, pt_ref, g_ref, b_ref, o_ref, ob_ref):
    y = _layer_norm(alpha * x_ref[...] + pt_ref[...].T, g_ref[...], b_ref[...])
    o_ref[...] = y
    ob_ref[...] = y.astype(BF16)


def _residual_ln(alpha, x, pt, g, b, tm, row0=0, n_tok=None):
    d = x.shape[1]
    n_tok = x.shape[0] if n_tok is None else n_tok
    blk0 = row0 // tm
    return pl.pallas_call(
        functools.partial(_residual_ln_kernel, alpha),
        grid=(n_tok // tm,),
        in_specs=[pl.BlockSpec((tm, d), lambda i: (blk0 + i, 0)),
                  pl.BlockSpec((d, tm), lambda i: (0, blk0 + i)),
                  pl.BlockSpec((1, d), lambda i: (0, 0)),
                  pl.BlockSpec((1, d), lambda i: (0, 0))],
        out_specs=[pl.BlockSpec((tm, d), lambda i: (i, 0))] * 2,
        out_shape=[jax.ShapeDtypeStruct((n_tok, d), F32), jax.ShapeDtypeStruct((n_tok, d), BF16)],
        compiler_params=_params(("parallel",), 40),
        name="residual_ln",
    )(x, pt, g, b)


def _block_diag(w, reps):
    return jnp.kron(jnp.eye(reps, dtype=w.dtype), w)


def kernel(x_prompt, x_sample, state_hgrn, w_in, sgu_ln_g, sgu_ln_b, sgu_w, sgu_b, hg_lb_logits,
           hg_norm_g, w_branch_a, w_branch_b, w_out, ln1_g, ln1_b, peer_w_q, peer_sub_keys,
           peer_u, peer_v, ln2_g, ln2_b):
    depth = w_in.shape[0]
    n_b, seq, d = x_prompt.shape
    n_s, dec_seq, _ = x_sample.shape
    width = sgu_ln_g.shape[1]
    n_prompt = n_b * seq
    alpha = (2 * depth) ** 0.25
    assert seq % ROWS == 0 and ROWS % dec_seq == 0 and n_s % (ROWS // dec_seq) == 0
    assert w_in.shape[2] == 6 * width + 2 * d and d == 2 * width
    assert n_prompt % 1024 == 0 and (n_s * dec_seq) % 1024 == 0

    xs = (x_prompt.reshape(n_prompt, d), x_sample.reshape(n_s * dec_seq, d))
    xs_b = tuple(a.astype(BF16) for a in xs)
    row = lambda a: a.reshape(1, -1)

    tri = jnp.tril(jnp.ones((ROWS, ROWS), F32))
    tri_s = _block_diag(jnp.tril(jnp.ones((dec_seq, dec_seq), F32)), ROWS // dec_seq)

    u_tab, vt_tab = _table_prep(peer_u, peer_v, 512)

    new_sp, new_ss, new_v = [], [], []
    for l in range(depth):
        z = _matmul(xs_b, w_in, l, 1024, 1024, "in_proj")

        ws = jnp.where(tri > 0, sgu_w[l], 0.0)
        ws_s = jax.vmap(lambda w: _block_diag(w, ROWS // dec_seq))(ws[:, :dec_seq, :dec_seq])
        bias = jnp.repeat(sgu_b[l].T, width // SGU_GROUPS, axis=1)
        bias_s = jnp.tile(bias[:dec_seq], (ROWS // dec_seq, 1))
        common = (row(sgu_ln_g[l]), row(sgu_ln_b[l]))
        tail = (hg_lb_logits, row(hg_norm_g[l]))
        ya_p, yb_p, st_p = _mixer_prompt(l, z, (n_b, seq), width, *common, ws.astype(BF16), bias,
                                         tri.astype(BF16), *tail)
        ya_s, yb_s, vn_s, st_s = _mixer_sample(l, z, n_prompt, (n_s, dec_seq), width, *common,
                                               ws_s.astype(BF16), bias_s, tri_s.astype(BF16), *tail,
                                               state_hgrn)
        ya = jnp.concatenate([ya_p, ya_s], axis=0)
        yb = jnp.concatenate([yb_p, yb_s], axis=0)

        x1, x1t = _merge(alpha, ya, yb, z, xs, w_branch_a[l].astype(BF16), w_branch_b[l].astype(BF16),
                         w_out[l].astype(BF16), row(ln1_g[l]), row(ln1_b[l]), 256,
                         (6 * width // d, 6 * width // d + 1))

        thr, c, s2, e2 = _peer_route(x1t, peer_w_q[l].T.astype(BF16), peer_sub_keys[l].astype(BF16), 256)
        pt = _peer_dense(l, x1t, u_tab, vt_tab, thr, c, s2, e2, 512, 1024)
        ln2 = (row(ln2_g[l]), row(ln2_b[l]), 512)
        y_p, yb_p = _residual_ln(alpha, x1, pt, *ln2, 0, n_prompt)
        y_s, yb_s = _residual_ln(alpha, x1, pt, *ln2, n_prompt, n_s * dec_seq)
        xs, xs_b = (y_p, y_s), (yb_p, yb_s)

        new_sp.append(st_p)
        new_ss.append(st_s)
        new_v.append(vn_s.reshape(n_s, dec_seq, width))

    return (xs[0].reshape(n_b, seq, d), xs[1].reshape(n_s, dec_seq, d),
            jnp.stack(new_sp), jnp.stack(new_ss), jnp.stack(new_v))
```

```python
import functools

import jax
import jax.numpy as jnp
from jax import lax
from jax.experimental import pallas as pl
from jax.experimental.pallas import tpu as pltpu

F32 = jnp.float32
BF16 = jnp.bfloat16

SGU_GROUPS = 4
HG_HEADS = 8
PEER_HEADS = 8
PEER_TOPK = 16
LN_EPS = 1e-5
RMS_EPS = 1e-6
ROWS = 128
HG_SUB = 16
MIB = 1024 * 1024

_NT = (((1,), (1,)), ((), ()))
_TN = (((0,), (0,)), ((), ()))


def _params(sem, vmem_mib):
    return pltpu.CompilerParams(dimension_semantics=sem, vmem_limit_bytes=vmem_mib * MIB)


def _gelu(x):
    c = 0.7978845608028654
    half = 0.5 * x
    return half + half * jnp.tanh(x * (c + (c * 0.044715) * (x * x)))


def _sigmoid(x):
    return 1.0 / (1.0 + jnp.exp(-x))


def _layer_norm(x, g, b):
    mu = jnp.mean(x, axis=-1, keepdims=True)
    xc = x - mu
    var = jnp.mean(xc * xc, axis=-1, keepdims=True)
    return xc * lax.rsqrt(var + LN_EPS) * g + b


def _bdot(a, b, dims=None):
    a = a.astype(BF16)
    b = b.astype(BF16)
    if dims is None:
        return jnp.dot(a, b, preferred_element_type=F32)
    return lax.dot_general(a, b, dims, preferred_element_type=F32)


def _two_group_specs(block, head_rows, tile, grid_axis):
    n_head = head_rows // tile
    pick = lambda ids: ids[grid_axis]
    return (n_head,
            pl.BlockSpec(block, lambda *ids: (jnp.minimum(pick(ids), n_head - 1), 0)),
            pl.BlockSpec(block, lambda *ids: (jnp.maximum(pick(ids) - n_head, 0), 0)))


def _matmul_kernel(n_head, xh_ref, xt_ref, w_ref, o_ref, wb_ref):
    i = pl.program_id(1)

    @pl.when(i == 0)
    def _():
        wb_ref[...] = w_ref[...].astype(BF16)

    @pl.when(i < n_head)
    def _():
        o_ref[...] = jnp.dot(xh_ref[...], wb_ref[...], preferred_element_type=F32)

    @pl.when(i >= n_head)
    def _():
        o_ref[...] = jnp.dot(xt_ref[...], wb_ref[...], preferred_element_type=F32)


def _matmul(xs, w, layer, tm, tn, name):
    k = xs[0].shape[1]
    m = xs[0].shape[0] + xs[1].shape[0]
    n = w.shape[2]
    n_head, head_spec, tail_spec = _two_group_specs((tm, k), xs[0].shape[0], tm, 1)
    return pl.pallas_call(
        functools.partial(_matmul_kernel, n_head),
        grid=(n // tn, m // tm),
        in_specs=[head_spec, tail_spec,
                  pl.BlockSpec((None, k, tn), lambda j, i: (layer, 0, j))],
        out_specs=pl.BlockSpec((tm, tn), lambda j, i: (i, j)),
        out_shape=jax.ShapeDtypeStruct((m, n), F32),
        scratch_shapes=[pltpu.VMEM((k, tn), BF16)],
        compiler_params=_params(("parallel", "arbitrary"), 56),
        name=name,
    )(xs[0], xs[1], w)


def _forget_lower_bound(logits_ref, layer):
    lg = logits_ref[...]
    e = jnp.exp(lg - jnp.max(lg, axis=0, keepdims=True))
    p = e / jnp.sum(e, axis=0, keepdims=True)
    return jnp.sum(p[1:layer + 1], axis=0, keepdims=True)


def _hgrn_gates(fp, lb):
    e = jnp.exp(-jnp.abs(fp))
    log_sig = jnp.minimum(fp, 0.0) - jnp.log1p(e)
    sig_neg = jnp.where(fp >= 0.0, e, 1.0) / (1.0 + e)
    if lb is None:
        return log_sig, sig_neg
    k = (1.0 - lb) * sig_neg
    pos = lb > 0.0
    a = jnp.log(jnp.where(pos, lb, 1.0))
    b = jnp.log1p(-lb) + log_sig
    lse = jnp.maximum(a, b) + jnp.log1p(jnp.exp(-jnp.abs(a - b)))
    return jnp.where(pos, lse, b), k


def _segment_cumsum(t_ref, x):
    t = t_ref[...]
    hi = x.astype(BF16)
    r1 = x - hi.astype(F32)
    mid = r1.astype(BF16)
    lo = (r1 - mid.astype(F32)).astype(BF16)
    dot = functools.partial(jnp.dot, preferred_element_type=F32)
    return dot(t, hi) + dot(t, mid) + dot(t, lo)


def _sgu(u, v, lng_ref, lnb_ref, ws_ref, bias_ref, ya_ref):
    vn = _layer_norm(_gelu(v), lng_ref[...], lnb_ref[...])
    vnb = vn.astype(BF16)
    gw = vn.shape[1] // SGU_GROUPS
    for g in range(SGU_GROUPS):
        cs = slice(g * gw, (g + 1) * gw)
        mixed = jnp.dot(ws_ref[g], vnb[:, cs], preferred_element_type=F32) + bias_ref[:, cs]
        ya_ref[:, cs] = (_gelu(u[:, cs]) * mixed).astype(BF16)
    return vn


def _hgrn_out(o, gn, og):
    o = o * lax.rsqrt(jnp.mean(o * o, axis=-1, keepdims=True) + RMS_EPS)
    return (o * gn * (og * _sigmoid(og))).astype(BF16)


def _rows_at(b_ref, size, offset, width):
    parts = []
    for j in range(ROWS // size):
        r = j * size + offset
        if r < 0:
            parts.append(jnp.zeros((size, width), F32))
        else:
            parts.append(jnp.broadcast_to(b_ref[pl.ds(r, 1), :], (size, width)))
    return parts[0] if len(parts) == 1 else jnp.concatenate(parts, axis=0)


def _mixer_prompt_kernel(layer, u_ref, v_ref, q_ref, f_ref, i_ref, og_ref, lng_ref, lnb_ref,
                         ws_ref, bias_ref, tri_ref, lbl_ref, gn_ref,
                         ya_ref, yb_ref, st_ref, stt_ref, b_ref):
    c = pl.program_id(1)
    width = q_ref.shape[1]
    dk = width // HG_HEADS

    @pl.when(c == 0)
    def _():
        stt_ref[...] = jnp.zeros_like(stt_ref)

    _sgu(u_ref[...], v_ref[...], lng_ref, lnb_ref, ws_ref, bias_ref, ya_ref)

    lb = None if layer == 0 else _forget_lower_bound(lbl_ref, layer)
    lf, k = _hgrn_gates(f_ref[...], lb)
    qs = q_ref[...] * (dk ** -0.5)
    b = _segment_cumsum(tri_ref, lf)
    b_ref[...] = b

    e_sub = b - _rows_at(b_ref, HG_SUB, -1, width)
    q_lvl = [qs * jnp.exp(e_sub)]
    k_lvl = [k * jnp.exp(-e_sub)]
    size = HG_SUB
    sizes = []
    while size < ROWS:
        sizes.append(size)
        q_lvl.append(q_lvl[0] if size == HG_SUB else qs * jnp.exp(b - _rows_at(b_ref, size, -1, width)))
        k_lvl.append(k * jnp.exp(_rows_at(b_ref, size, size - 1, width) - b))
        size *= 2
    qd = qs * jnp.exp(b)
    b_end = b_ref[pl.ds(ROWS - 1, 1), :]
    ke = k * jnp.exp(b_end - b)
    g_end = jnp.exp(b_end)

    row = lax.broadcasted_iota(jnp.int32, (ROWS, ROWS), 0)
    col = lax.broadcasted_iota(jnp.int32, (ROWS, ROWS), 1)
    masks = [(row // HG_SUB == col // HG_SUB) & (col <= row)]
    for sz in sizes:
        masks.append(((row // sz) % 2 == 1) & (col // sz == row // sz - 1))

    vv = i_ref[...]
    gn = gn_ref[...]
    og = og_ref[...]
    for h in range(HG_HEADS):
        hs = slice(h * dk, (h + 1) * dk)
        attn = jnp.zeros((ROWS, ROWS), F32)
        for ql, kl, mk in zip(q_lvl, k_lvl, masks):
            attn = jnp.where(mk, _bdot(ql[:, hs], kl[:, hs], _NT), attn)
        vh = vv[:, hs]
        stt = stt_ref[h]
        o = _bdot(attn, vh) + _bdot(qd[:, hs], stt, _NT)
        stt_ref[h] = stt * g_end[:, hs] + _bdot(vh, ke[:, hs], _TN)
        yb_ref[:, hs] = _hgrn_out(o, gn[:, hs], og[:, hs])

    @pl.when(c == pl.num_programs(1) - 1)
    def _():
        for h in range(HG_HEADS):
            st_ref[0, h] = stt_ref[h].T


def _mixer_sample_kernel(layer, seq_len, u_ref, v_ref, q_ref, f_ref, i_ref, og_ref, lng_ref, lnb_ref,
                         ws_ref, bias_ref, tri_ref, lbl_ref, gn_ref, s0_ref,
                         ya_ref, yb_ref, vn_ref, st_ref, b_ref, oi_ref):
    width = q_ref.shape[1]
    dk = width // HG_HEADS
    n_seq = ROWS // seq_len

    vn_ref[...] = _sgu(u_ref[...], v_ref[...], lng_ref, lnb_ref, ws_ref, bias_ref, ya_ref)

    lb = None if layer == 0 else _forget_lower_bound(lbl_ref, layer)
    lf, k = _hgrn_gates(f_ref[...], lb)
    qs = q_ref[...] * (dk ** -0.5)
    b = _segment_cumsum(tri_ref, lf)
    b_ref[...] = b
    qd = qs * jnp.exp(b)
    k0 = k * jnp.exp(-b)
    ke = k * jnp.exp(_rows_at(b_ref, seq_len, seq_len - 1, width) - b)

    row = lax.broadcasted_iota(jnp.int32, (ROWS, ROWS), 0)
    col = lax.broadcasted_iota(jnp.int32, (ROWS, ROWS), 1)
    causal = (row // seq_len == col // seq_len) & (col <= row)
    seq_of_row = lax.broadcasted_iota(jnp.int32, (ROWS, dk), 0) // seq_len

    vv = i_ref[...]
    gn = gn_ref[...]
    og = og_ref[...]
    for h in range(HG_HEADS):
        hs = slice(h * dk, (h + 1) * dk)
        attn = jnp.where(causal, _bdot(qd[:, hs], k0[:, hs], _NT), 0.0)
        vh = vv[:, hs]
        qdh = qd[:, hs].astype(BF16)
        keh = ke[:, hs].astype(BF16)
        oi_ref[...] = _bdot(attn, vh)

        def per_seq(s, carry, h=h, hs=hs, vh=vh, qdh=qdh, keh=keh):
            mine = seq_of_row == s
            s0 = s0_ref[s, h]
            oi_ref[...] += jnp.where(mine, _bdot(qdh, s0), 0.0)
            b_seq = b_ref[pl.ds(pl.multiple_of(s * seq_len, seq_len), seq_len), hs]
            g_end = jnp.exp(b_seq[seq_len - 1:seq_len])
            new_t = s0.T * g_end + _bdot(jnp.where(mine, vh, 0.0), keh, _TN)
            st_ref[s, h] = new_t.T
            return carry

        lax.fori_loop(0, n_seq, per_seq, 0, unroll=4)
        yb_ref[:, hs] = _hgrn_out(oi_ref[...], gn[:, hs], og[:, hs])


def _mixer_specs(width, row_block, n_axes):
    def idx(f):
        return f if n_axes == 2 else (lambda t: f(t, 0))

    def zcol(j):
        return pl.BlockSpec((ROWS, width), idx(lambda a, c, j=j: (row_block(a, c), j)))

    def const(shape):
        return pl.BlockSpec(shape, idx(lambda a, c: (0,) * len(shape)))

    return zcol, const


def _mixer_prompt(layer, z, batch, width, lng, lnb, ws, bias, tri, lbl, gn):
    n_chunks = batch[1] // ROWS
    zcol, const = _mixer_specs(width, lambda a, c: a * n_chunks + c, 2)
    depth = lbl.shape[0]
    dk = width // HG_HEADS
    n_tok = batch[0] * batch[1]
    return pl.pallas_call(
        functools.partial(_mixer_prompt_kernel, layer),
        grid=(batch[0], n_chunks),
        in_specs=[zcol(j) for j in range(6)] + [
            const((1, width)), const((1, width)), const((SGU_GROUPS, ROWS, ROWS)),
            const((ROWS, width)), const((ROWS, ROWS)), const((depth, width)), const((1, width))],
        out_specs=[pl.BlockSpec((ROWS, width), lambda a, c: (a * n_chunks + c, 0)),
                   pl.BlockSpec((ROWS, width), lambda a, c: (a * n_chunks + c, 0)),
                   pl.BlockSpec((1, HG_HEADS, dk, dk), lambda a, c: (a, 0, 0, 0))],
        out_shape=[jax.ShapeDtypeStruct((n_tok, width), BF16),
                   jax.ShapeDtypeStruct((n_tok, width), BF16),
                   jax.ShapeDtypeStruct((batch[0], HG_HEADS, dk, dk), F32)],
        scratch_shapes=[pltpu.VMEM((HG_HEADS, dk, dk), F32), pltpu.VMEM((ROWS, width), F32)],
        compiler_params=_params(("parallel", "arbitrary"), 56),
        name="mixer_prompt",
    )(z, z, z, z, z, z, lng, lnb, ws, bias, tri, lbl, gn)


def _mixer_sample(layer, z, row0, batch, width, lng, lnb, ws, bias, tri, lbl, gn, s0):
    n_seq, seq_len = batch
    per_step = ROWS // seq_len
    steps = n_seq // per_step
    blk0 = row0 // ROWS
    zcol, const = _mixer_specs(width, lambda t, c: blk0 + t, 1)
    depth = lbl.shape[0]
    dk = width // HG_HEADS
    n_tok = n_seq * seq_len
    st_spec = pl.BlockSpec((per_step, HG_HEADS, dk, dk), lambda t: (t, 0, 0, 0))
    row_spec = pl.BlockSpec((ROWS, width), lambda t: (t, 0))
    return pl.pallas_call(
        functools.partial(_mixer_sample_kernel, layer, seq_len),
        grid=(steps,),
        in_specs=[zcol(j) for j in range(6)] + [
            const((1, width)), const((1, width)), const((SGU_GROUPS, ROWS, ROWS)),
            const((ROWS, width)), const((ROWS, ROWS)), const((depth, width)), const((1, width)),
            pl.BlockSpec((None, per_step, HG_HEADS, dk, dk), lambda t: (layer, t, 0, 0, 0))],
        out_specs=[row_spec, row_spec, row_spec, st_spec],
        out_shape=[jax.ShapeDtypeStruct((n_tok, width), BF16),
                   jax.ShapeDtypeStruct((n_tok, width), BF16),
                   jax.ShapeDtypeStruct((n_tok, width), F32),
                   jax.ShapeDtypeStruct((n_seq, HG_HEADS, dk, dk), F32)],
        scratch_shapes=[pltpu.VMEM((ROWS, width), F32), pltpu.VMEM((ROWS, dk), F32)],
        compiler_params=_params(("parallel",), 58),
        name="mixer_sample",
    )(z, z, z, z, z, z, lng, lnb, ws, bias, tri, lbl, gn, s0)


def _merge_kernel(alpha, n_head, ya_ref, yb_ref, ga_ref, gb_ref, xh_ref, xt_ref, wpa_ref, wpb_ref, wo_ref,
                  g_ref, b_ref, x1_ref, x1t_ref):
    a = jnp.dot(ya_ref[...], wpa_ref[...], preferred_element_type=F32)
    b = jnp.dot(yb_ref[...], wpb_ref[...], preferred_element_type=F32)
    merged = _sigmoid(ga_ref[...]) * a + _sigmoid(gb_ref[...]) * b
    y = jnp.dot(merged.astype(BF16), wo_ref[...], preferred_element_type=F32)
    x = jnp.where(pl.program_id(0) < n_head, xh_ref[...], xt_ref[...])
    x1 = _layer_norm(alpha * x + y, g_ref[...], b_ref[...])
    x1_ref[...] = x1
    x1t_ref[...] = x1.T.astype(BF16)


def _merge(alpha, ya, yb, z, xs, wpa, wpb, wo, g, b, tm, gate_blocks):
    d = xs[0].shape[1]
    n_tok = xs[0].shape[0] + xs[1].shape[0]
    da = ya.shape[1]
    ga_blk, gb_blk = gate_blocks
    const = lambda shape: pl.BlockSpec(shape, lambda i: (0,) * len(shape), pipeline_mode=pl.Buffered(1))
    n_head, head_spec, tail_spec = _two_group_specs((tm, d), xs[0].shape[0], tm, 0)
    return pl.pallas_call(
        functools.partial(_merge_kernel, alpha, n_head),
        grid=(n_tok // tm,),
        in_specs=[pl.BlockSpec((tm, da), lambda i: (i, 0)),
                  pl.BlockSpec((tm, da), lambda i: (i, 0)),
                  pl.BlockSpec((tm, d), lambda i: (i, ga_blk)),
                  pl.BlockSpec((tm, d), lambda i: (i, gb_blk)),
                  head_spec, tail_spec,
                  const((da, d)), const((da, d)), const((d, d)), const((1, d)), const((1, d))],
        out_specs=[pl.BlockSpec((tm, d), lambda i: (i, 0)),
                   pl.BlockSpec((d, tm), lambda i: (0, i))],
        out_shape=[jax.ShapeDtypeStruct((n_tok, d), F32),
                   jax.ShapeDtypeStruct((d, n_tok), BF16)],
        compiler_params=_params(("parallel",), 58),
        name="merge_ln",
    )(ya, yb, z, z, xs[0], xs[1], wpa, wpb, wo, g, b)


def _compare_exchange(v, i, l):
    v[i], v[l] = jnp.maximum(v[i], v[l]), jnp.minimum(v[i], v[l])


def _bitonic_merge(v):
    n = len(v)
    j = n // 2
    while j:
        for i in range(n):
            if i ^ j > i:
                _compare_exchange(v, i, i ^ j)
        j //= 2
    return v


def _top_sorted(a, keep=PEER_TOPK):
    sub = 8
    n = a.shape[0] // sub
    assert n & (n - 1) == 0 and 2 * n >= keep
    v = [a[sub * k:sub * (k + 1)] for k in range(n)]
    k = 2
    while k <= n:
        j = k // 2
        while j:
            for i in range(n):
                l = i ^ j
                if l > i:
                    if i & k == 0:
                        _compare_exchange(v, i, l)
                    else:
                        _compare_exchange(v, l, i)
            j //= 2
        k *= 2
    shift = sub // 2
    while shift:
        other = [pltpu.roll(t, shift, 0) for t in v]
        m = len(v)
        if m < keep:
            v = _bitonic_merge(v + other[::-1])
        else:
            v = _bitonic_merge([jnp.maximum(v[i], other[m - 1 - i]) for i in range(m)])
        shift //= 2
    return [t[0:1] for t in v[:keep]]


def _stack_rows(rows, n):
    lanes = rows[0].shape[1]
    idx = lax.broadcasted_iota(jnp.int32, (n, lanes), 0)
    acc = jnp.broadcast_to(rows[0], (n, lanes))
    for r in range(1, n):
        acc = jnp.where(idx == r, rows[r], acc)
    return acc


def _peer_route_kernel(xt_ref, wqt_ref, sk_ref, thr_ref, c_ref, s2_ref, e2_ref, q_ref):
    q_ref[...] = jnp.dot(wqt_ref[...], xt_ref[...], preferred_element_type=F32).astype(BF16)
    half = sk_ref.shape[3]

    def per_head(h, carry):
        base = pl.multiple_of(h * 2 * half, 2 * half)
        s1 = jnp.dot(sk_ref[h, 0], q_ref[pl.ds(base, half), :], preferred_element_type=F32)
        s2 = jnp.dot(sk_ref[h, 1], q_ref[pl.ds(base + half, half), :], preferred_element_type=F32)
        t1 = _top_sorted(s1)
        t2 = _top_sorted(s2)
        t2_all = _stack_rows(t2, PEER_TOPK)
        t2_half = t2_all[:PEER_TOPK // 2]
        sums = [t1[a] + t2[b] for a in range(3, PEER_TOPK) for b in range(PEER_TOPK // (a + 1))]
        sums += [jnp.full_like(t1[0], -jnp.inf)] * (-len(sums) % 8)
        packed = [_stack_rows(sums[r:r + 8], 8) for r in range(0, len(sums), 8)]
        filler = jnp.full_like(t2_half, -jnp.inf)
        cand = jnp.concatenate([t1[0] + t2_all, t1[1] + t2_half, t1[2] + t2_half] + packed + [filler], axis=0)
        tau = _top_sorted(cand)[PEER_TOPK - 1]
        rows = [t1[a] + t2_all for a in range(PEER_TOPK)]
        count = lambda mask: jnp.sum(jnp.where(mask, 1.0, 0.0), axis=0, keepdims=True)
        n_gt = [count(r > tau) for r in rows]
        n_eq = [count(r == tau) for r in rows]
        spare = float(PEER_TOPK) - sum(n_gt)
        rank = lax.broadcasted_iota(jnp.int32, t2_all.shape, 0).astype(F32)
        top_score = t1[0] + t2[0]
        z = jnp.zeros_like(tau)
        thr = jnp.full(s1.shape, jnp.inf, F32)
        for a in range(PEER_TOPK):
            take = jnp.minimum(n_eq[a], spare)
            spare = spare - take
            keep = rank < n_gt[a] + take
            z = z + jnp.sum(jnp.where(keep, jnp.exp(rows[a] - top_score), 0.0), axis=0, keepdims=True)
            lowest_kept = jnp.min(jnp.where(keep, t2_all, jnp.inf), axis=0, keepdims=True)
            thr = jnp.where(s1 == t1[a], lowest_kept, thr)
        thr_ref[h] = thr
        c_ref[h] = jnp.exp(s1 - t1[0]) / z
        s2_ref[h] = s2
        e2_ref[h] = jnp.exp(s2 - t2[0])
        return carry

    lax.fori_loop(0, PEER_HEADS, per_head, 0)


def _peer_route(xt, wqt, sk, tb):
    d, n_tok = xt.shape
    nk = sk.shape[2]
    out = jax.ShapeDtypeStruct((PEER_HEADS, nk, n_tok), F32)
    ospec = pl.BlockSpec((PEER_HEADS, nk, tb), lambda i: (0, 0, i))
    return pl.pallas_call(
        _peer_route_kernel,
        grid=(n_tok // tb,),
        in_specs=[pl.BlockSpec((d, tb), lambda i: (0, i)),
                  pl.BlockSpec(wqt.shape, lambda i: (0, 0)),
                  pl.BlockSpec(sk.shape, lambda i: (0, 0, 0, 0))],
        out_specs=[ospec] * 4,
        out_shape=[out] * 4,
        scratch_shapes=[pltpu.VMEM((wqt.shape[0], tb), BF16)],
        compiler_params=_params(("parallel",), 48),
        name="peer_route",
    )(xt, wqt, sk)


def _peer_dense_kernel(n_tiles, n_exp_tiles, lane_chunk, xt_ref, u_ref, vt_ref, thr_ref, c_ref,
                       s2_ref, e2_ref, out_ref, ht_a, ht_b, pt_a, pt_b):
    s = pl.program_id(0)
    retrieved = jnp.clip(s - 2, 0, n_tiles - 1)

    @pl.when(s == 0)
    def _():
        for ref in (ht_a, ht_b, pt_a, pt_b):
            ref[...] = jnp.zeros_like(ref)

    @pl.when(retrieved % n_exp_tiles == 0)
    def _():
        out_ref[...] = jnp.zeros_like(out_ref)

    nk = s2_ref.shape[1]
    tb = xt_ref.shape[1]
    n_keys = u_ref.shape[0] // nk
    sub = 16

    def stages(ht_new, ht_old, pt_new, pt_old):
        def project(rs, ls):
            ht_new[rs, ls] = jnp.dot(u_ref[rs, :], xt_ref[:, ls], preferred_element_type=F32)

        def gate(keys, j0, ls):
            acc = {}
            for h in range(PEER_HEADS):
                s2 = s2_ref[h, j0:j0 + sub, ls]
                e2 = e2_ref[h, j0:j0 + sub, ls]
                for ii in keys:
                    hit = s2 >= thr_ref[h, ii:ii + 1, ls]
                    term = jnp.where(hit, e2, 0.0) * c_ref[h, ii:ii + 1, ls]
                    acc[ii] = term if h == 0 else acc[ii] + term
            for ii in keys:
                rs = slice(ii * nk + j0, ii * nk + j0 + sub)
                pt_new[rs, ls] = (acc[ii] * _gelu(ht_old[rs, ls])).astype(BF16)

        def retrieve(rs, ls):
            out_ref[rs, ls] += jnp.dot(vt_ref[rs, :], pt_old[:, ls], preferred_element_type=F32)

        lanes = [slice(t0, t0 + lane_chunk) for t0 in range(0, tb, lane_chunk)]
        blk = 128
        p_units = [functools.partial(project, slice(r0, r0 + blk), ls)
                   for ls in lanes for r0 in range(0, u_ref.shape[0], blk)]
        r_units = [functools.partial(retrieve, slice(r0, r0 + blk), ls)
                   for ls in lanes for r0 in range(0, vt_ref.shape[0], blk)]
        key_groups = [range(n_keys)]
        g_units = [functools.partial(gate, keys, j0, slice(g0, g0 + 128))
                   for g0 in range(0, tb, 128) for j0 in range(0, nk, sub) for keys in key_groups]
        for n, g_unit in enumerate(g_units):
            for units in (p_units, r_units):
                if n * len(units) % len(g_units) == 0:
                    units[n * len(units) // len(g_units)]()
            g_unit()

    @pl.when(s % 2 == 0)
    def _():
        stages(ht_a, ht_b, pt_b, pt_a)

    @pl.when(s % 2 == 1)
    def _():
        stages(ht_b, ht_a, pt_a, pt_b)


def _table_prep_kernel(u_ref, v_ref, ub_ref, vt_ref):
    ub_ref[...] = u_ref[...].astype(BF16)
    vt_ref[...] = v_ref[...].T.astype(BF16)


def _table_prep(peer_u, peer_v, eb, tile):
    depth, n_exp, d = peer_u.shape
    per = tile // eb
    in_spec = pl.BlockSpec((None, eb, d), lambda l, e: (l, e, 0))
    return pl.pallas_call(
        _table_prep_kernel,
        grid=(depth, n_exp // eb),
        in_specs=[in_spec, in_spec],
        out_specs=[pl.BlockSpec((None, eb, d), lambda l, e: (l, e, 0)),
                   pl.BlockSpec((None, None, d, eb), lambda l, e: (l, e // per, 0, e % per))],
        out_shape=[jax.ShapeDtypeStruct((depth, n_exp, d), BF16),
                   jax.ShapeDtypeStruct((depth, n_exp // tile, d, tile), BF16)],
        compiler_params=_params(("parallel", "parallel"), 48),
        name="table_prep",
    )(peer_u, peer_v)


def _peer_dense(layer, xt, u, vt, thr, c, s2, e2, tb, eb):
    d, n_tok = xt.shape
    n_exp = u.shape[1]
    nk = s2.shape[1]
    keys_per_blk = eb // nk
    n_e = n_exp // eb
    n_tiles = (n_tok // tb) * n_e
    proj = lambda s: jnp.minimum(s, n_tiles - 1)
    gate = lambda s: jnp.clip(s - 1, 0, n_tiles - 1)
    retr = lambda s: jnp.clip(s - 2, 0, n_tiles - 1)
    row_spec = pl.BlockSpec((PEER_HEADS, keys_per_blk, tb), lambda s: (0, gate(s) % n_e, gate(s) // n_e))
    all_spec = pl.BlockSpec((PEER_HEADS, nk, tb), lambda s: (0, 0, gate(s) // n_e))
    return pl.pallas_call(
        functools.partial(_peer_dense_kernel, n_tiles, n_e, 256),
        grid=(n_tiles + 2,),
        in_specs=[pl.BlockSpec((d, tb), lambda s: (0, proj(s) // n_e)),
                  pl.BlockSpec((None, eb, d), lambda s: (layer, proj(s) % n_e, 0)),
                  pl.BlockSpec((None, None, d, eb), lambda s: (layer, retr(s) % n_e, 0, 0)),
                  row_spec, row_spec, all_spec, all_spec],
        out_specs=pl.BlockSpec((d, tb), lambda s: (0, retr(s) // n_e)),
        out_shape=jax.ShapeDtypeStruct((d, n_tok), F32),
        scratch_shapes=[pltpu.VMEM((eb, tb), F32), pltpu.VMEM((eb, tb), F32),
                        pltpu.VMEM((eb, tb), BF16), pltpu.VMEM((eb, tb), BF16)],
        compiler_params=_params(("arbitrary",), 58),
        name="peer_dense",
    )(xt, u, vt, thr, c, s2, e2)


def _residual_ln_kernel(alpha, x_ref, pt_ref, g_ref, b_ref, o_ref, ob_ref):
    y = _layer_norm(alpha * x_ref[...] + pt_ref[...].T, g_ref[...], b_ref[...])
    o_ref[...] = y
    ob_ref[...] = y.astype(BF16)


def _residual_ln(alpha, x, pt, g, b, tm, row0=0, n_tok=None):
    d = x.shape[1]
    n_tok = x.shape[0] if n_tok is None else n_tok
    blk0 = row0 // tm
    return pl.pallas_call(
        functools.partial(_residual_ln_kernel, alpha),
        grid=(n_tok // tm,),
        in_specs=[pl.BlockSpec((tm, d), lambda i: (blk0 + i, 0)),
                  pl.BlockSpec((d, tm), lambda i: (0, blk0 + i)),
                  pl.BlockSpec((1, d), lambda i: (0, 0)),
                  pl.BlockSpec((1, d), lambda i: (0, 0))],
        out_specs=[pl.BlockSpec((tm, d), lambda i: (i, 0))] * 2,
        out_shape=[jax.ShapeDtypeStruct((n_tok, d), F32), jax.ShapeDtypeStruct((n_tok, d), BF16)],
        compiler_params=_params(("parallel",), 40),
        name="residual_ln",
    )(x, pt, g, b)


def _block_diag(w, reps):
    return jnp.kron(jnp.eye(reps, dtype=w.dtype), w)


def kernel(x_prompt, x_sample, state_hgrn, w_in, sgu_ln_g, sgu_ln_b, sgu_w, sgu_b, hg_lb_logits,
           hg_norm_g, w_branch_a, w_branch_b, w_out, ln1_g, ln1_b, peer_w_q, peer_sub_keys,
           peer_u, peer_v, ln2_g, ln2_b):
    depth = w_in.shape[0]
    n_b, seq, d = x_prompt.shape
    n_s, dec_seq, _ = x_sample.shape
    width = sgu_ln_g.shape[1]
    n_prompt = n_b * seq
    alpha = (2 * depth) ** 0.25
    assert seq % ROWS == 0 and ROWS % dec_seq == 0 and n_s % (ROWS // dec_seq) == 0
    assert w_in.shape[2] == 6 * width + 2 * d and d == 2 * width
    assert n_prompt % 1024 == 0 and (n_s * dec_seq) % 1024 == 0

    xs = (x_prompt.reshape(n_prompt, d), x_sample.reshape(n_s * dec_seq, d))
    xs_b = tuple(a.astype(BF16) for a in xs)
    row = lambda a: a.reshape(1, -1)

    tri = jnp.tril(jnp.ones((ROWS, ROWS), F32))
    tri_s = _block_diag(jnp.tril(jnp.ones((dec_seq, dec_seq), F32)), ROWS // dec_seq)

    u_tab, vt_tab = _table_prep(peer_u, peer_v, 512, 1024)

    new_sp, new_ss, new_v = [], [], []
    for l in range(depth):
        z = _matmul(xs_b, w_in, l, 1024, 1024, "in_proj")

        ws = jnp.where(tri > 0, sgu_w[l], 0.0)
        ws_s = jax.vmap(lambda w: _block_diag(w, ROWS // dec_seq))(ws[:, :dec_seq, :dec_seq])
        bias = jnp.repeat(sgu_b[l].T, width // SGU_GROUPS, axis=1)
        bias_s = jnp.tile(bias[:dec_seq], (ROWS // dec_seq, 1))
        common = (row(sgu_ln_g[l]), row(sgu_ln_b[l]))
        tail = (hg_lb_logits, row(hg_norm_g[l]))
        ya_p, yb_p, st_p = _mixer_prompt(l, z, (n_b, seq), width, *common, ws.astype(BF16), bias,
                                         tri.astype(BF16), *tail)
        ya_s, yb_s, vn_s, st_s = _mixer_sample(l, z, n_prompt, (n_s, dec_seq), width, *common,
                                               ws_s.astype(BF16), bias_s, tri_s.astype(BF16), *tail,
                                               state_hgrn)
        ya = jnp.concatenate([ya_p, ya_s], axis=0)
        yb = jnp.concatenate([yb_p, yb_s], axis=0)

        x1, x1t = _merge(alpha, ya, yb, z, xs, w_branch_a[l].astype(BF16), w_branch_b[l].astype(BF16),
                         w_out[l].astype(BF16), row(ln1_g[l]), row(ln1_b[l]), 256,
                         (6 * width // d, 6 * width // d + 1))

        thr, c, s2, e2 = _peer_route(x1t, peer_w_q[l].T.astype(BF16), peer_sub_keys[l].astype(BF16), 256)
        pt = _peer_dense(l, x1t, u_tab, vt_tab, thr, c, s2, e2, 512, 1024)
        ln2 = (row(ln2_g[l]), row(ln2_b[l]), 512)
        y_p, yb_p = _residual_ln(alpha, x1, pt, *ln2, 0, n_prompt)
        y_s, yb_s = _residual_ln(alpha, x1, pt, *ln2, n_prompt, n_s * dec_seq)
        xs, xs_b = (y_p, y_s), (yb_p, yb_s)

        new_sp.append(st_p)
        new_ss.append(st_s)
        new_v.append(vn_s.reshape(n_s, dec_seq, width))

    return (xs[0].reshape(n_b, seq, d), xs[1].reshape(n_s, dec_seq, d),
            jnp.stack(new_sp), jnp.stack(new_ss), jnp.stack(new_v))
```
